```python
import math
import jax
import jax.numpy as jnp
from jax import lax
import numpy as np

D_MODEL = 2048
BATCH = 4
SEQ = 4096
DEPTH = 4

GRID_W = 64
CTX_LEN = 256
NORM_EPS = 1e-5
N_ADA = 6

MIX_WIDTH = D_MODEL // 2

HY_WIDTH = MIX_WIDTH
HY_BANDS = 16
HY_EMB = 1 + 2 * HY_BANDS
HY_FFN = 64
HY_DECAY_SLOW = math.log(1e-2) / 1.5
HY_DECAY_FAST = math.log(1e-2) / 0.3
HY_FILTER_STD = 0.02

ATT_HEAD_DIM = 64
ATT_HEADS = MIX_WIDTH // ATT_HEAD_DIM
ATT_KV_HEADS = 2
ATT_GROUP = ATT_HEADS // ATT_KV_HEADS
WINDOW = 128
ATT_BLOCK = 128
ROPE_BASE = 10000.0
NEG_INF = -1e30

GLA_HEADS = 4
GLA_DV = MIX_WIDTH // GLA_HEADS
GLA_DK = GLA_DV // 2
GLA_RANK = 16
GLA_TAU = 16.0
GLA_CHUNK = 64

N_BRANCH = 3

IN_COLS = (3 * HY_WIDTH,
           ATT_HEADS * ATT_HEAD_DIM,
           ATT_KV_HEADS * ATT_HEAD_DIM,
           ATT_KV_HEADS * ATT_HEAD_DIM,
           GLA_HEADS * GLA_DK,
           GLA_HEADS * GLA_DK,
           GLA_HEADS * GLA_DV,
           GLA_HEADS * GLA_DV,
           2 * GLA_RANK,
           N_BRANCH * D_MODEL)
IN_OFFSETS = tuple(sum(IN_COLS[:i + 1]) for i in range(len(IN_COLS) - 1))
P_IN = sum(IN_COLS)

N_EXPERTS = 32
TOP_K = 4
EXPERT_FF = D_MODEL // 4
SWIGLU_LIMIT = 7.0
SWIGLU_ALPHA = 1.702
MOE_BLOCK = 128

kernel_name = 'hybrid_dit_hyena_swa_gla_moe'


def rms_norm(x, w):
    xf = x.astype(jnp.float32)
    y = xf * lax.rsqrt(jnp.mean(xf * xf, axis=-1, keepdims=True) + NORM_EPS)
    return y.astype(x.dtype) * w


def modulate(h, shift, scale):
    return h * (1 + scale) + shift


def adaln_params(cond, w, b):
    m = jax.nn.silu(cond) @ w + b
    return [t[..., None, :] for t in jnp.split(m, N_ADA, axis=-1)]


def split_heads(t, n_heads, head_dim):
    return t.reshape(*t.shape[:-1], n_heads, head_dim)


def short_conv3(u, w, b):
    up = jnp.pad(u, ((0, 0), (1, 1), (0, 0)))
    return up[:, :-2] * w[0] + up[:, 1:-1] * w[1] + up[:, 2:] * w[2] + b


def hyena_filter(L, w1, b1, freq, w2, b2, w3, decay):
    f32 = jnp.float32
    pos = jnp.arange(L, dtype=f32)
    t = pos / max(L - 1, 1)
    ang = 2.0 * math.pi * pos / L
    bands = jnp.linspace(1e-4, HY_BANDS - 1, HY_BANDS, dtype=f32)
    fw = ang[:, None] * bands[None, :]
    z = jnp.concatenate([t[:, None], jnp.cos(fw), -jnp.sin(fw)], axis=-1)
    freq = freq.astype(f32)
    h = jnp.sin(freq * (z @ w1.astype(f32) + b1.astype(f32)))
    h = jnp.sin(freq * (h @ w2.astype(f32) + b2.astype(f32)))
    h = (h @ w3.astype(f32)) * jnp.exp(-t[:, None] * jnp.abs(decay.astype(f32)))
    h_fwd, h_bwd = h[:, :HY_WIDTH], h[:, HY_WIDTH:]
    return jnp.concatenate([h_fwd, jnp.zeros((1, HY_WIDTH), f32), h_bwd[:0:-1]], axis=0)


def bidir_fftconv(u, kern, bias):
    L = u.shape[1]
    uf32 = u.astype(jnp.float32)
    uf = jnp.fft.rfft(uf32, n=2 * L, axis=1)
    kf = jnp.fft.rfft(kern, axis=0)
    y = jnp.fft.irfft(uf * kf[None], n=2 * L, axis=1)[:, :L]
    return (y + uf32 * bias.astype(jnp.float32)).astype(u.dtype)


def hyena_mixer(p, conv_w, conv_b, w1, b1, freq, w2, b2, w3, decay, bias):
    u = short_conv3(p, conv_w, conv_b)
    x0, x1, v = jnp.split(u, 3, axis=-1)
    kern = hyena_filter(p.shape[1], w1, b1, freq, w2, b2, w3, decay)
    return x0 * bidir_fftconv(x1 * v, kern, bias)


def axial_rope_tables(n_tokens):
    f32 = jnp.float32
    rows = n_tokens // GRID_W
    row_pos = jnp.broadcast_to(jnp.arange(rows, dtype=f32)[:, None], (rows, GRID_W)).reshape(-1)
    col_pos = jnp.broadcast_to(jnp.arange(GRID_W, dtype=f32)[None, :], (rows, GRID_W)).reshape(-1)
    n_freq = ATT_HEAD_DIM // 4
    inv = ROPE_BASE ** (-jnp.arange(n_freq, dtype=f32) / n_freq)
    ang = jnp.stack([row_pos[:, None] * inv, col_pos[:, None] * inv], axis=1)
    return jnp.cos(ang), jnp.sin(ang)


def apply_axial_rope(x, cos, sin):
    xs = x.astype(jnp.float32).reshape(*x.shape[:-1], 2, 2, ATT_HEAD_DIM // 4)
    x1, x2 = xs[..., 0, :], xs[..., 1, :]
    c, s = cos[None, :, None], sin[None, :, None]
    out = jnp.stack([x1 * c - x2 * s, x2 * c + x1 * s], axis=-2)
    return out.reshape(x.shape).astype(x.dtype)


def windowed_attention(q, k, v, kc, vc, sink):
    B, N, H, hd = q.shape
    nb = N // ATT_BLOCK
    qb = q.reshape(B, nb, ATT_BLOCK, ATT_KV_HEADS, ATT_GROUP, hd)

    def band(t):
        tp = jnp.pad(t, ((0, 0), (ATT_BLOCK, ATT_BLOCK), (0, 0), (0, 0)))
        tp = tp.reshape(B, nb + 2, ATT_BLOCK, ATT_KV_HEADS, hd)
        return jnp.concatenate([tp[:, :-2], tp[:, 1:-1], tp[:, 2:]], axis=2)

    kb, vb = band(k), band(v)
    scale = hd ** -0.5
    s_loc = jnp.einsum('bnqkgd,bnskd->bnkgqs', qb, kb).astype(jnp.float32) * scale
    q_pos = jnp.arange(N).reshape(nb, ATT_BLOCK)
    k_pos = (jnp.arange(nb)[:, None] - 1) * ATT_BLOCK + jnp.arange(3 * ATT_BLOCK)[None, :]
    kp = k_pos[:, None, :]
    allowed = (kp >= 0) & (kp < N) & (jnp.abs(q_pos[:, :, None] - kp) <= WINDOW)
    s_loc = jnp.where(allowed[None, :, None, None], s_loc, NEG_INF)
    s_ctx = jnp.einsum('bnqkgd,bckd->bnkgqc', qb, kc).astype(jnp.float32) * scale
    s_sink = jnp.broadcast_to(sink.astype(jnp.float32).reshape(1, 1, ATT_KV_HEADS, ATT_GROUP, 1, 1),
                              s_loc.shape[:-1] + (1,))
    probs = jax.nn.softmax(jnp.concatenate([s_loc, s_ctx, s_sink], axis=-1), axis=-1).astype(v.dtype)
    n_loc, n_ctx = 3 * ATT_BLOCK, kc.shape[1]
    o = (jnp.einsum('bnkgqs,bnskd->bnqkgd', probs[..., :n_loc], vb)
         + jnp.einsum('bnkgqc,bckd->bnqkgd', probs[..., n_loc:n_loc + n_ctx], vc))
    return o.reshape(B, N, H * hd)


def context_attention(qc, kc, vc, sink):
    B, C, H, hd = qc.shape
    qg = qc.reshape(B, C, ATT_KV_HEADS, ATT_GROUP, hd)
    s = jnp.einsum('bqkgd,bckd->bkgqc', qg, kc).astype(jnp.float32) * hd ** -0.5
    s_sink = jnp.broadcast_to(sink.astype(jnp.float32).reshape(1, ATT_KV_HEADS, ATT_GROUP, 1, 1),
                              s.shape[:-1] + (1,))
    probs = jax.nn.softmax(jnp.concatenate([s, s_sink], axis=-1), axis=-1)[..., :C].astype(vc.dtype)
    o = jnp.einsum('bkgqc,bckd->bqkgd', probs, vc)
    return o.reshape(B, C, H * hd)


def gla_prepare(pq, pk, pv, pa, a_w, a_b):
    B, L, _ = pq.shape
    f32 = jnp.float32
    q = pq.astype(f32).reshape(B, L, GLA_HEADS, GLA_DK) * GLA_DK ** -0.5
    k = pk.astype(f32).reshape(B, L, GLA_HEADS, GLA_DK)
    v = pv.astype(f32).reshape(B, L, GLA_HEADS, GLA_DV)
    z = jnp.einsum('blxr,xrk->xblk', pa.astype(f32).reshape(B, L, 2, GLA_RANK), a_w.astype(f32))
    z = z + a_b.astype(f32)[:, None, None, :]
    log_a = (jax.nn.log_sigmoid(z) / GLA_TAU).reshape(2, B, L, GLA_HEADS, GLA_DK)
    return q, k, v, log_a


def gla_chunked(q, k, v, log_a, s0):
    B, L, H, dk = q.shape
    dv = v.shape[-1]
    n = L // GLA_CHUNK

    def chunks(t):
        return t.reshape(B, n, GLA_CHUNK, H, t.shape[-1])

    q, k, v, log_a = chunks(q), chunks(k), chunks(v), chunks(log_a)
    b = jnp.cumsum(log_a, axis=2)
    b_last = b[:, :, -1:]
    q_in = q * jnp.exp(b)
    k_in = k * jnp.exp(-b)
    k_state = k * jnp.exp(b_last - b)
    lower = jnp.tril(jnp.ones((GLA_CHUNK, GLA_CHUNK), dtype=bool))
    att = jnp.where(lower, jnp.einsum('bnthd,bnshd->bnhts', q_in, k_in), 0.0)
    o_intra = jnp.einsum('bnhts,bnshv->bnthv', att, v)
    kv = jnp.einsum('bnshd,bnshv->bnhdv', k_state, v)
    decay = jnp.exp(b_last[:, :, 0])

    def step(S, inp):
        dec, kv_c = inp
        return dec[..., None] * S + kv_c, S

    s_final, s_start = lax.scan(step, s0, (jnp.moveaxis(decay, 1, 0), jnp.moveaxis(kv, 1, 0)))
    s_start = jnp.moveaxis(s_start, 0, 1)
    o_inter = jnp.einsum('bnthd,bnhdv->bnthv', q_in, s_start)
    return (o_intra + o_inter).reshape(B, L, H, dv), s_final


def gla_bidir(q, k, v, log_a, s0_fwd, s0_bwd):
    o_f, s_f = gla_chunked(q, k, v, log_a[0], s0_fwd)
    o_b, s_b = gla_chunked(jnp.flip(q, 1), jnp.flip(k, 1), jnp.flip(v, 1), jnp.flip(log_a[1], 1), s0_bwd)
    return o_f + jnp.flip(o_b, 1), s_f, s_b


def gla_output(o, r, norm_w):
    B, L = o.shape[:2]
    o = o * lax.rsqrt(jnp.mean(o * o, axis=-1, keepdims=True) + NORM_EPS) * norm_w.astype(jnp.float32)
    return o.reshape(B, L, GLA_HEADS * GLA_DV).astype(r.dtype) * jax.nn.silu(r)


def merge_branches(y_a, y_b, y_c, gate_logits, w_branch, w_out):
    B, L, _ = gate_logits.shape
    y = jnp.stack([y_a, y_b, y_c], axis=2)
    proj = jnp.einsum('blgc,gcd->blgd', y, w_branch)
    gates = jax.nn.sigmoid(gate_logits.reshape(B, L, N_BRANCH, D_MODEL))
    return jnp.sum(gates * proj, axis=2) @ w_out


def moe_ffn(h, router_w, router_b, w1, b1, w2, b2):
    T, D = h.shape
    logits = (h @ router_w + router_b).astype(jnp.float32)
    top_logit, top_idx = lax.top_k(logits, TOP_K)
    gate = jax.nn.softmax(top_logit, axis=-1).astype(h.dtype)
    n_assign = T * TOP_K
    flat_e = top_idx.reshape(-1)
    flat_t = jnp.repeat(jnp.arange(T, dtype=jnp.int32), TOP_K)
    order = jnp.argsort(flat_e)
    e_s, t_s, g_s = flat_e[order], flat_t[order], gate.reshape(-1)[order]
    counts = jnp.bincount(flat_e, length=N_EXPERTS)
    padded = (counts + MOE_BLOCK - 1) // MOE_BLOCK * MOE_BLOCK
    ends_pad = jnp.cumsum(padded)
    start = jnp.cumsum(counts) - counts
    dest = (ends_pad - padded)[e_s] + jnp.arange(n_assign, dtype=jnp.int32) - start[e_s]
    n_blocks = -(-(n_assign + N_EXPERTS * (MOE_BLOCK - 1)) // MOE_BLOCK)
    n_rows = n_blocks * MOE_BLOCK
    row_tok = jnp.zeros((n_rows,), jnp.int32).at[dest].set(t_s)
    row_gate = jnp.zeros((n_rows,), h.dtype).at[dest].set(g_s)
    block_e = jnp.minimum(jnp.searchsorted(ends_pad, jnp.arange(n_blocks, dtype=jnp.int32) * MOE_BLOCK,
                                           side='right'), N_EXPERTS - 1)

    def expert_block(args):
        e, tok, g = args
        a = h[tok] @ w1[e] + b1[e]
        a_glu, a_lin = jnp.split(a, 2, axis=-1)
        a_glu = jnp.minimum(a_glu, SWIGLU_LIMIT)
        a_lin = jnp.clip(a_lin, -SWIGLU_LIMIT, SWIGLU_LIMIT)
        act = a_glu * jax.nn.sigmoid(SWIGLU_ALPHA * a_glu) * (a_lin + 1)
        return (act @ w2[e] + b2[e]) * g[:, None]

    y = lax.map(expert_block, (block_e, row_tok.reshape(n_blocks, MOE_BLOCK),
                               row_gate.reshape(n_blocks, MOE_BLOCK)))
    return jnp.zeros_like(h).at[row_tok].add(y.reshape(n_rows, D))


def setup_inputs(seed: int = 0) -> dict:
    key = jax.random.key(seed)
    keys = jax.random.split(key, 40)
    counter = [0]

    def nrm(shape, scale):
        k = keys[counter[0]]
        counter[0] += 1
        return jax.random.normal(k, shape, jnp.float32) * scale

    D = D_MODEL
    decay_base = jnp.tile(jnp.linspace(HY_DECAY_SLOW, HY_DECAY_FAST, HY_WIDTH, dtype=jnp.float32), 2)
    return {
        'x': nrm((BATCH, SEQ, D), 1.0),
        'c': nrm((BATCH, D), 1.0),
        'ctx': nrm((BATCH, CTX_LEN, D), 1.0),
        'c_ctx': nrm((D,), 1.0),
        'ada_w': nrm((DEPTH, D, N_ADA * D), 0.5 * D ** -0.5),
        'ada_b': nrm((DEPTH, N_ADA * D), 0.02),
        'norm1_w': 1.0 + nrm((DEPTH, D), 0.1),
        'norm2_w': 1.0 + nrm((DEPTH, D), 0.1),
        'w_in': nrm((DEPTH, D, P_IN), D ** -0.5),
        'hy_conv_w': nrm((DEPTH, 3, 3 * HY_WIDTH), 3 ** -0.5),
        'hy_conv_b': nrm((DEPTH, 3 * HY_WIDTH), 0.02),
        'hy_filt_w1': nrm((DEPTH, HY_EMB, HY_FFN), HY_EMB ** -0.5),
        'hy_filt_b1': nrm((DEPTH, HY_FFN), 0.1),
        'hy_filt_freq': 1.0 + nrm((DEPTH, HY_FFN), 0.1),
        'hy_filt_w2': nrm((DEPTH, HY_FFN, HY_FFN), HY_FFN ** -0.5),
        'hy_filt_b2': nrm((DEPTH, HY_FFN), 0.1),
        'hy_filt_w3': nrm((DEPTH, HY_FFN, 2 * HY_WIDTH), HY_FILTER_STD),
        'hy_decay': decay_base[None, :] + nrm((DEPTH, 2 * HY_WIDTH), 0.1),
        'hy_bias': nrm((DEPTH, HY_WIDTH), 1.0),
        'attn_sink': nrm((DEPTH, ATT_HEADS), 1.0),
        'gla_a_w': nrm((DEPTH, 2, GLA_RANK, GLA_HEADS * GLA_DK), GLA_RANK ** -0.5),
        'gla_a_b': nrm((DEPTH, 2, GLA_HEADS * GLA_DK), 0.1),
        'gla_norm_w': 1.0 + nrm((DEPTH, GLA_DV), 0.1),
        'w_branch': nrm((DEPTH, N_BRANCH, MIX_WIDTH, D), MIX_WIDTH ** -0.5),
        'w_out': nrm((DEPTH, D, D), D ** -0.5),
        'router_w': nrm((DEPTH, D, N_EXPERTS), D ** -0.5),
        'router_b': nrm((DEPTH, N_EXPERTS), 0.01),
        'moe_w1': nrm((DEPTH, N_EXPERTS, D, 2 * EXPERT_FF), D ** -0.5),
        'moe_b1': nrm((DEPTH, N_EXPERTS, 2 * EXPERT_FF), 0.02),
        'moe_w2': nrm((DEPTH, N_EXPERTS, EXPERT_FF, D), EXPERT_FF ** -0.5),
        'moe_b2': nrm((DEPTH, N_EXPERTS, D), 0.02),
        'final_norm_w': 1.0 + nrm((D,), 0.1),
    }


def reference(x, c, ctx, c_ctx, ada_w, ada_b, norm1_w, norm2_w, w_in, hy_conv_w, hy_conv_b,
              hy_filt_w1, hy_filt_b1, hy_filt_freq, hy_filt_w2, hy_filt_b2, hy_filt_w3, hy_decay,
              hy_bias, attn_sink, gla_a_w, gla_a_b, gla_norm_w, w_branch, w_out, router_w, router_b,
              moe_w1, moe_b1, moe_w2, moe_b2, final_norm_w):
    B, N, _ = x.shape
    cos, sin = axial_rope_tables(N)
    xc = ctx
    for l in range(DEPTH):
        need_ctx = l < DEPTH - 1
        sh1, sc1, g1, sh2, sc2, g2 = adaln_params(c, ada_w[l], ada_b[l])
        csh1, csc1, cg1, csh2, csc2, cg2 = adaln_params(c_ctx, ada_w[l], ada_b[l])
        h = modulate(rms_norm(x, norm1_w[l]), sh1, sc1)
        hc = modulate(rms_norm(xc, norm1_w[l]), csh1, csc1)
        p = jnp.split(h @ w_in[l], IN_OFFSETS, axis=-1)
        pc = jnp.split(hc @ w_in[l], IN_OFFSETS, axis=-1)
        hy_params = (hy_conv_w[l], hy_conv_b[l], hy_filt_w1[l], hy_filt_b1[l], hy_filt_freq[l],
                     hy_filt_w2[l], hy_filt_b2[l], hy_filt_w3[l], hy_decay[l], hy_bias[l])
        y_a = hyena_mixer(p[0], *hy_params)
        q = apply_axial_rope(split_heads(p[1], ATT_HEADS, ATT_HEAD_DIM), cos, sin)
        k = apply_axial_rope(split_heads(p[2], ATT_KV_HEADS, ATT_HEAD_DIM), cos, sin)
        v = split_heads(p[3], ATT_KV_HEADS, ATT_HEAD_DIM)
        kc = split_heads(pc[2], ATT_KV_HEADS, ATT_HEAD_DIM)
        vc = split_heads(pc[3], ATT_KV_HEADS, ATT_HEAD_DIM)
        y_b = windowed_attention(q, k, v, kc, vc, attn_sink[l])
        gq, gk, gv, g_la = gla_prepare(p[4], p[5], p[6], p[8], gla_a_w[l], gla_a_b[l])
        cq, ck, cv, c_la = gla_prepare(pc[4], pc[5], pc[6], pc[8], gla_a_w[l], gla_a_b[l])
        zero_state = jnp.zeros((B, GLA_HEADS, GLA_DK, GLA_DV), jnp.float32)
        o_ctx, s_fwd, s_bwd = gla_bidir(cq, ck, cv, c_la, zero_state, zero_state)
        o_lat, _, _ = gla_bidir(gq, gk, gv, g_la, s_fwd, s_bwd)
        y_c = gla_output(o_lat, p[7], gla_norm_w[l])
        x = x + g1 * merge_branches(y_a, y_b, y_c, p[9], w_branch[l], w_out[l])
        if need_ctx:
            y_ac = hyena_mixer(pc[0], *hy_params)
            y_bc = context_attention(split_heads(pc[1], ATT_HEADS, ATT_HEAD_DIM), kc, vc, attn_sink[l])
            y_cc = gla_output(o_ctx, pc[7], gla_norm_w[l])
            xc = xc + cg1 * merge_branches(y_ac, y_bc, y_cc, pc[9], w_branch[l], w_out[l])
        moe_params = (router_w[l], router_b[l], moe_w1[l], moe_b1[l], moe_w2[l], moe_b2[l])
        h2 = modulate(rms_norm(x, norm2_w[l]), sh2, sc2).reshape(B * N, D_MODEL)
        if need_ctx:
            h2c = modulate(rms_norm(xc, norm2_w[l]), csh2, csc2).reshape(-1, D_MODEL)
            f = moe_ffn(jnp.concatenate([h2, h2c], axis=0), *moe_params)
            x = x + g2 * f[:B * N].reshape(B, N, D_MODEL)
            xc = xc + cg2 * f[B * N:].reshape(xc.shape)
        else:
            x = x + g2 * moe_ffn(h2, *moe_params).reshape(B, N, D_MODEL)
    return rms_norm(x, final_norm_w)
```

```python
import functools
import math

import jax
import jax.numpy as jnp
import numpy as np
from jax import lax
from jax.experimental import pallas as pl
from jax.experimental.pallas import tpu as pltpu

BF16 = jnp.bfloat16
F32 = jnp.float32

D_MODEL = 2048
DEPTH = 4
GRID_W = 64
NORM_EPS = 1e-5
N_ADA = 6
MIX_WIDTH = D_MODEL // 2

HY_WIDTH = MIX_WIDTH
HY_BANDS = 16
HY_EMB = 1 + 2 * HY_BANDS
HY_FFN = 64

ATT_HEAD_DIM = 64
ATT_HEADS = MIX_WIDTH // ATT_HEAD_DIM
ATT_KV_HEADS = 2
ATT_GROUP = ATT_HEADS // ATT_KV_HEADS
WINDOW = 128
ATT_BLOCK = 128
ROPE_BASE = 10000.0
NEG_INF = -1e30

GLA_HEADS = 4
GLA_DV = MIX_WIDTH // GLA_HEADS
GLA_DK = GLA_DV // 2
GLA_RANK = 16
GLA_TAU = 16.0
GLA_CHUNK = 64

N_BRANCH = 3
N_EXPERTS = 32
TOP_K = 4
EXPERT_FF = D_MODEL // 4
SWIGLU_LIMIT = 7.0
SWIGLU_ALPHA = 1.702

LANES = 128
SUBLANES = 8

COL_HY_X0 = 0
COL_HY_X1 = 1024
COL_HY_V = 2048
COL_ATT_Q = 3072
COL_GLA_V = 4096
COL_GLA_R = 5120
COL_GATES = 6144
COL_GLA_Q = 12288
COL_GLA_K = 12800
COL_ATT_K = 13312
COL_ATT_V = 13440
COL_DECAY = 13568
P_PAD = 13824


def _cparams(dims, vmem_mb=48):
    return pltpu.CompilerParams(dimension_semantics=dims, vmem_limit_bytes=vmem_mb << 20)


def _split_bf16(a):
    hi = a.astype(BF16)
    lo = (a - hi.astype(F32)).astype(BF16)
    return hi, lo


def _ada_kernel(c_ref, w_ref, b_ref, o_ref):
    cv = c_ref[...]
    s_hi, s_lo = _split_bf16(cv * jax.nn.sigmoid(cv))
    w_hi, w_lo = _split_bf16(w_ref[...])
    acc = jnp.dot(s_hi, w_hi, preferred_element_type=F32)
    acc += jnp.dot(s_lo, w_hi, preferred_element_type=F32)
    acc += jnp.dot(s_hi, w_lo, preferred_element_type=F32)
    o_ref[...] = acc + b_ref[...]


def _ada_call(cond8, ada_w, ada_b):
    depth, d, n6 = ada_w.shape
    tn = 1024
    return pl.pallas_call(
        _ada_kernel,
        grid=(depth, n6 // tn),
        in_specs=[
            pl.BlockSpec((SUBLANES, d), lambda l, j: (0, 0)),
            pl.BlockSpec((None, d, tn), lambda l, j: (l, 0, j)),
            pl.BlockSpec((None, 1, tn), lambda l, j: (l, 0, j)),
        ],
        out_specs=pl.BlockSpec((None, SUBLANES, tn), lambda l, j: (l, 0, j)),
        out_shape=jax.ShapeDtypeStruct((depth, SUBLANES, n6), F32),
        compiler_params=_cparams(("arbitrary", "arbitrary")),
        name="ada",
    )(cond8, ada_w, ada_b.reshape(depth, 1, n6))


def _rms_modulate(xf, nw, sh, sc1):
    ms = jnp.mean(xf * xf, axis=-1, keepdims=True)
    return (xf * lax.rsqrt(ms + NORM_EPS) * nw) * sc1 + sh


def _in_kernel(x_ref, nw_ref, sh_ref, sc_ref, w_ref, o_ref, h_ref, *, tm, rc):
    @pl.when(pl.program_id(1) == 0)
    def _():
        nw = nw_ref[...]
        sh = sh_ref[...]
        sc1 = 1.0 + sc_ref[...]

        def body(r, carry):
            sl = pl.ds(pl.multiple_of(r * rc, rc), rc)
            h_ref[sl, :] = _rms_modulate(x_ref[sl, :], nw, sh, sc1).astype(BF16)
            return carry

        lax.fori_loop(0, tm // rc, body, 0)

    o_ref[...] = jnp.dot(h_ref[...], w_ref[...], preferred_element_type=F32).astype(o_ref.dtype)


def _token_tile(n_lat_per_batch, n_ctx_total, cap):
    tm = cap
    while n_lat_per_batch % tm or n_ctx_total % tm:
        tm //= 2
    return tm


def _mod_row(i, tiles_per_batch, n_lat_tiles, n_batch):
    return jnp.where(i < n_lat_tiles, i // tiles_per_batch, n_batch)


def _in_call(xf, norm_w, mods_l, w_bf, *, n_batch, n_seq):
    t_rows, d = xf.shape
    p_pad = w_bf.shape[1]
    tm = _token_tile(n_seq, t_rows - n_batch * n_seq, 1024)
    tn = 512
    tpb = n_seq // tm
    nlt = n_batch * tpb

    def mod_spec(comp):
        return pl.BlockSpec((None, None, 1, d), lambda i, j: (_mod_row(i, tpb, nlt, n_batch), comp, 0, 0))

    return pl.pallas_call(
        functools.partial(_in_kernel, tm=tm, rc=min(tm, 64)),
        grid=(t_rows // tm, p_pad // tn),
        in_specs=[
            pl.BlockSpec((tm, d), lambda i, j: (i, 0)),
            pl.BlockSpec((1, d), lambda i, j: (0, 0)),
            mod_spec(0),
            mod_spec(1),
            pl.BlockSpec((d, tn), lambda i, j: (0, j)),
        ],
        out_specs=pl.BlockSpec((tm, tn), lambda i, j: (i, j)),
        out_shape=jax.ShapeDtypeStruct((t_rows, p_pad), BF16),
        scratch_shapes=[pltpu.VMEM((tm, d), BF16)],
        compiler_params=_cparams(("arbitrary", "arbitrary")),
        name="in_proj",
    )(xf, norm_w.reshape(1, d), mods_l, mods_l, w_bf)


def _rope_tables(n_tokens):
    pos = np.arange(n_tokens)
    row_pos = (pos // GRID_W).astype(np.float32)
    col_pos = (pos % GRID_W).astype(np.float32)
    n_freq = ATT_HEAD_DIM // 4
    inv = (ROPE_BASE ** (-np.arange(n_freq, dtype=np.float32) / n_freq)).astype(np.float32)
    lane = np.arange(LANES)
    dd = lane % ATT_HEAD_DIM
    axis = dd // (2 * n_freq)
    freq = dd % n_freq
    second = (dd % (2 * n_freq)) >= n_freq
    p = jnp.where(jnp.asarray(axis)[None, :] == 0, jnp.asarray(row_pos)[:, None], jnp.asarray(col_pos)[:, None])
    ang = p * jnp.asarray(inv)[jnp.asarray(freq)][None, :]
    sign = jnp.where(jnp.asarray(second), 1.0, -1.0).astype(F32)
    return jnp.cos(ang), jnp.sin(ang) * sign[None, :]


def _rope_tile(xf, cos, sin_signed):
    lane = lax.broadcasted_iota(jnp.int32, xf.shape, 1)
    first = (lane % 32) < 16
    partner = jnp.where(first, pltpu.roll(xf, LANES - 16, 1), pltpu.roll(xf, 16, 1))
    return xf * cos + partner * sin_signed


def _rope_k_kernel(k_ref, cos_ref, sin_ref, o_ref):
    o_ref[...] = _rope_tile(k_ref[...].astype(F32), cos_ref[...], sin_ref[...]).astype(BF16)


def _rope_k_call(p, cos_t, sin_t, *, n_batch, n_seq):
    tr = min(n_seq, 512)
    nb = n_seq // tr
    return pl.pallas_call(
        _rope_k_kernel,
        grid=(n_batch * nb,),
        in_specs=[
            pl.BlockSpec((tr, LANES), lambda i: (i, COL_ATT_K // LANES)),
            pl.BlockSpec((tr, LANES), lambda i: (i % nb, 0)),
            pl.BlockSpec((tr, LANES), lambda i: (i % nb, 0)),
        ],
        out_specs=pl.BlockSpec((tr, LANES), lambda i: (i, 0)),
        out_shape=jax.ShapeDtypeStruct((n_batch * n_seq, LANES), BF16),
        compiler_params=_cparams(("arbitrary",)),
        name="rope_k",
    )(p, cos_t, sin_t)


def _attn_kernel(*refs, has_local, n_seq):
    if has_local:
        (sink_ref, q_ref, kp_ref, kc_ref, kn_ref, vp_ref, vc_ref, vn_ref, kx_ref, vx_ref,
         cos_ref, sin_ref, o_ref) = refs
    else:
        sink_ref, q_ref, kx_ref, vx_ref, o_ref = refs
    i = pl.program_id(1)
    hd = ATT_HEAD_DIM
    if has_local:
        qpos = i * ATT_BLOCK + lax.broadcasted_iota(jnp.int32, (ATT_BLOCK, 3 * ATT_BLOCK), 0)
        kpos = (i - 1) * ATT_BLOCK + lax.broadcasted_iota(jnp.int32, (ATT_BLOCK, 3 * ATT_BLOCK), 1)
        allowed = (kpos >= 0) & (kpos < n_seq) & (jnp.abs(qpos - kpos) <= WINDOW)
        kloc = jnp.concatenate([kp_ref[...], kc_ref[...], kn_ref[...]], axis=0)
        vloc = jnp.concatenate([vp_ref[...], vc_ref[...], vn_ref[...]], axis=0)
        cos = cos_ref[...]
        sin = sin_ref[...]
    kx = kx_ref[...]
    vx = vx_ref[...]
    nt = (((1,), (1,)), ((), ()))
    for g in range(ATT_KV_HEADS):
        gs = slice(g * hd, (g + 1) * hd)
        kx_g = kx[:, gs]
        vx_g = vx[:, gs]
        if has_local:
            kloc_g = kloc[:, gs]
            vloc_g = vloc[:, gs]
        for pr in range(ATT_GROUP // 2):
            hp = g * (ATT_GROUP // 2) + pr
            qf = q_ref[:, hp * LANES:(hp + 1) * LANES].astype(F32)
            if has_local:
                qf = _rope_tile(qf, cos, sin)
            qb = (qf * (hd ** -0.5)).astype(BF16)
            outs = []
            for hf in range(2):
                h = hp * 2 + hf
                qh = qb[:, hf * hd:(hf + 1) * hd]
                sink = sink_ref[h]
                s_x = lax.dot_general(qh, kx_g, nt, preferred_element_type=F32)
                m = jnp.maximum(jnp.max(s_x, axis=-1, keepdims=True), sink)
                if has_local:
                    s_l = lax.dot_general(qh, kloc_g, nt, preferred_element_type=F32)
                    s_l = jnp.where(allowed, s_l, NEG_INF)
                    m = jnp.maximum(m, jnp.max(s_l, axis=-1, keepdims=True))
                e_x = jnp.exp(s_x - m)
                den = jnp.sum(e_x, axis=-1, keepdims=True) + jnp.exp(sink - m)
                acc = jnp.dot(e_x.astype(BF16), vx_g, preferred_element_type=F32)
                if has_local:
                    e_l = jnp.exp(s_l - m)
                    den = den + jnp.sum(e_l, axis=-1, keepdims=True)
                    acc = acc + jnp.dot(e_l.astype(BF16), vloc_g, preferred_element_type=F32)
                outs.append(acc / den)
            o_ref[:, hp * LANES:(hp + 1) * LANES] = jnp.concatenate(outs, axis=1).astype(o_ref.dtype)


def _attn_call(sink, q_arr, q_col, q_row0, k_arr, k_col, v_arr, v_col, p, cos_t, sin_t, *, n_batch, n_seq, n_ctx,
               ctx_row0, has_local):
    nq = n_seq // ATT_BLOCK
    qb0 = q_row0 // ATT_BLOCK
    kcb = COL_ATT_K // LANES
    vcb = COL_ATT_V // LANES
    cb0 = ctx_row0 // n_ctx

    def loc(col, d):
        return pl.BlockSpec(
            (ATT_BLOCK, LANES),
            lambda b, i: (qb0 + b * nq + jnp.clip(i + d, 0, nq - 1), col))

    in_specs = [pl.BlockSpec(memory_space=pltpu.SMEM),
                pl.BlockSpec((ATT_BLOCK, MIX_WIDTH), lambda b, i: (qb0 + b * nq + i, q_col))]
    args = [sink, q_arr]
    if has_local:
        in_specs += [loc(k_col, -1), loc(k_col, 0), loc(k_col, 1), loc(v_col, -1), loc(v_col, 0), loc(v_col, 1)]
        args += [k_arr, k_arr, k_arr, v_arr, v_arr, v_arr]
    in_specs += [pl.BlockSpec((n_ctx, LANES), lambda b, i: (cb0 + b, kcb)),
                 pl.BlockSpec((n_ctx, LANES), lambda b, i: (cb0 + b, vcb))]
    args += [p, p]
    if has_local:
        in_specs += [pl.BlockSpec((ATT_BLOCK, LANES), lambda b, i: (i, 0)),
                     pl.BlockSpec((ATT_BLOCK, LANES), lambda b, i: (i, 0))]
        args += [cos_t, sin_t]
    return pl.pallas_call(
        functools.partial(_attn_kernel, has_local=has_local, n_seq=n_seq),
        grid=(n_batch, nq),
        in_specs=in_specs,
        out_specs=pl.BlockSpec((ATT_BLOCK, MIX_WIDTH), lambda b, i: (b * nq + i, 0)),
        out_shape=jax.ShapeDtypeStruct((n_batch * n_seq, MIX_WIDTH), BF16),
        compiler_params=_cparams(("arbitrary", "arbitrary")),
        name="attn_local" if has_local else "attn_ctx",
    )(*args)


def _gla_kernel(*refs, reverse, fuse_out, tb):
    if fuse_out:
        (q_ref, k_ref, v_ref, pa_ref, aw_ref, ab_ref, s0_ref, oo_ref, r_ref, nw_ref,
         y_ref, sout_ref, st_ref, la_ref) = refs
    else:
        q_ref, k_ref, v_ref, pa_ref, aw_ref, ab_ref, s0_ref, y_ref, sout_ref, st_ref, la_ref = refs
    i = pl.program_id(1)
    ck = GLA_CHUNK
    nchunk = tb // ck

    @pl.when(i == 0)
    def _():
        st_ref[...] = s0_ref[...]

    z = jnp.dot(pa_ref[...], aw_ref[...], preferred_element_type=F32) + ab_ref[...]
    la_ref[...] = (jnp.minimum(z, 0.0) - jnp.log(1.0 + jnp.exp(-jnp.abs(z)))) * (1.0 / GLA_TAU)

    r_i = lax.broadcasted_iota(jnp.int32, (ck, ck), 0)
    c_i = lax.broadcasted_iota(jnp.int32, (ck, ck), 1)
    tri = (c_i >= r_i) if reverse else (c_i <= r_i)
    tri_b = jnp.where(tri, 1.0, 0.0).astype(BF16)
    nt = (((1,), (1,)), ((), ()))
    tn = (((0,), (0,)), ((), ()))
    if fuse_out:
        nw = nw_ref[...]

    def chunk(ci, carry):
        c = (nchunk - 1 - ci) if reverse else ci
        rs = pl.ds(pl.multiple_of(c * ck, ck), ck)
        la_hi, la_lo = _split_bf16(la_ref[rs, :])
        b = jnp.dot(tri_b, la_hi, preferred_element_type=F32) + jnp.dot(tri_b, la_lo, preferred_element_type=F32)
        b_tot = b[0:1, :] if reverse else b[ck - 1:ck, :]
        kf = k_ref[rs, :].astype(F32)
        q_in = (q_ref[rs, :].astype(F32) * (GLA_DK ** -0.5) * jnp.exp(b)).astype(BF16)
        k_in = (kf * jnp.exp(-b)).astype(BF16)
        k_st = (kf * jnp.exp(b_tot - b)).astype(BF16)
        dec = jnp.exp(b_tot)
        vv = v_ref[rs, :]
        for h in range(GLA_HEADS):
            hs = slice(h * GLA_DK, (h + 1) * GLA_DK)
            vs = slice(h * GLA_DV, (h + 1) * GLA_DV)
            att = lax.dot_general(q_in[:, hs], k_in[:, hs], nt, preferred_element_type=F32)
            att = jnp.where(tri, att, 0.0).astype(BF16)
            st = st_ref[h]
            o = jnp.dot(att, vv[:, vs], preferred_element_type=F32)
            o = o + lax.dot_general(q_in[:, hs], st.astype(BF16), nt, preferred_element_type=F32)
            st_ref[h] = st * dec[:, hs] + lax.dot_general(vv[:, vs], k_st[:, hs], tn, preferred_element_type=F32)
            if fuse_out:
                tot = o + oo_ref[rs, vs]
                yn = tot * lax.rsqrt(jnp.mean(tot * tot, axis=-1, keepdims=True) + NORM_EPS) * nw
                rf = r_ref[rs, vs].astype(F32)
                y_ref[rs, vs] = (yn * (rf * jax.nn.sigmoid(rf))).astype(y_ref.dtype)
            else:
                y_ref[rs, vs] = o
        return carry

    lax.fori_loop(0, nchunk, chunk, 0)

    @pl.when(i == pl.num_programs(1) - 1)
    def _():
        sout_ref[...] = st_ref[...]


def _gla_call(p, aw_pad, ab, s0, o_other, norm_w, *, n_batch, n_seq, row0, reverse):
    fuse_out = o_other is not None
    tb = min(n_seq, 512)
    nblk = n_seq // tb
    rb0 = row0 // tb

    def blk(i):
        return (nblk - 1 - i) if reverse else i

    def prow(width, col):
        return pl.BlockSpec((tb, width), lambda b, i: (rb0 + b * nblk + blk(i), col // width))

    def orow(width):
        return pl.BlockSpec((tb, width), lambda b, i: (b * nblk + blk(i), 0))

    st_spec = pl.BlockSpec((None, GLA_HEADS, GLA_DV, GLA_DK), lambda b, i: (b, 0, 0, 0))
    nq = GLA_HEADS * GLA_DK
    in_specs = [prow(nq, COL_GLA_Q), prow(nq, COL_GLA_K), prow(MIX_WIDTH, COL_GLA_V), prow(LANES, COL_DECAY),
                pl.BlockSpec((LANES, nq), lambda b, i: (0, 0)), pl.BlockSpec((1, nq), lambda b, i: (0, 0)), st_spec]
    args = [p, p, p, p, aw_pad, ab, s0]
    if fuse_out:
        in_specs += [orow(MIX_WIDTH), prow(MIX_WIDTH, COL_GLA_R), pl.BlockSpec((1, GLA_DV), lambda b, i: (0, 0))]
        args += [o_other, p, norm_w.reshape(1, GLA_DV)]
    return pl.pallas_call(
        functools.partial(_gla_kernel, reverse=reverse, fuse_out=fuse_out, tb=tb),
        grid=(n_batch, nblk),
        in_specs=in_specs,
        out_specs=[orow(MIX_WIDTH), st_spec],
        out_shape=[jax.ShapeDtypeStruct((n_batch * n_seq, MIX_WIDTH), BF16 if fuse_out else F32),
                   jax.ShapeDtypeStruct((n_batch, GLA_HEADS, GLA_DV, GLA_DK), F32)],
        scratch_shapes=[pltpu.VMEM((GLA_HEADS, GLA_DV, GLA_DK), F32), pltpu.VMEM((tb, nq), F32)],
        compiler_params=_cparams(("arbitrary", "arbitrary")),
        name="gla_" + ("bwd" if reverse else "fwd") + ("_out" if fuse_out else ""),
    )(*args)


def _gla_decay_weights(a_w, a_b):
    nq = GLA_HEADS * GLA_DK
    pads = []
    for d in range(2):
        z = jnp.zeros((LANES, nq), F32).at[d * GLA_RANK:(d + 1) * GLA_RANK].set(a_w[d])
        pads.append(z.astype(BF16))
    return pads, [a_b[0].reshape(1, nq), a_b[1].reshape(1, nq)]


def _gla_layer(p, a_w, a_b, norm_w, *, n_batch, n_seq, n_ctx, need_ctx):
    (aw_f, aw_b), (ab_f, ab_b) = _gla_decay_weights(a_w, a_b)
    t_lat = n_batch * n_seq
    zero = jnp.zeros((n_batch, GLA_HEADS, GLA_DV, GLA_DK), F32)
    kw_c = dict(n_batch=n_batch, n_seq=n_ctx, row0=t_lat)
    kw_l = dict(n_batch=n_batch, n_seq=n_seq, row0=0)
    oc_b, s_b = _gla_call(p, aw_b, ab_b, zero, None, None, reverse=True, **kw_c)
    if need_ctx:
        y_cc, s_f = _gla_call(p, aw_f, ab_f, zero, oc_b, norm_w, reverse=False, **kw_c)
    else:
        y_cc = None
        _, s_f = _gla_call(p, aw_f, ab_f, zero, None, None, reverse=False, **kw_c)
    ol_b, _ = _gla_call(p, aw_b, ab_b, s_b, None, None, reverse=True, **kw_l)
    y_c, _ = _gla_call(p, aw_f, ab_f, s_f, ol_b, norm_w, reverse=False, **kw_l)
    return y_c, y_cc


HY_N2 = 128
HY_J = SUBLANES
HY_G = HY_N2 // HY_J
HY_CB = 128
HY_TILE = 2 * HY_J


def _dot3(a, b):
    a_hi, a_lo = _split_bf16(a)
    b_hi, b_lo = _split_bf16(b)
    acc = jnp.dot(a_hi, b_hi, preferred_element_type=F32)
    acc += jnp.dot(a_lo, b_hi, preferred_element_type=F32)
    acc += jnp.dot(a_hi, b_lo, preferred_element_type=F32)
    return acc


def _hy_filter_kernel(z_ref, w1_ref, b1_ref, fr_ref, w2_ref, b2_ref, w3_ref, dec_ref, o_ref):
    z = z_ref[...]
    fr = fr_ref[...]
    h = jnp.sin(fr * (_dot3(z, w1_ref[...]) + b1_ref[...]))
    h = jnp.sin(fr * (_dot3(h, w2_ref[...]) + b2_ref[...]))
    t = z[:, 0:1]
    o_ref[...] = _dot3(h, w3_ref[...]) * jnp.exp(-t * jnp.abs(dec_ref[...]))


def _hy_filter_call(seq_len, w1, b1, freq, w2, b2, w3, decay):
    pos = np.arange(seq_len, dtype=np.float32)
    t = pos / np.float32(max(seq_len - 1, 1))
    ang = (np.float32(2.0 * math.pi) * pos / np.float32(seq_len)).astype(np.float32)
    bands = np.linspace(1e-4, HY_BANDS - 1, HY_BANDS, dtype=np.float32)
    fw = (ang[:, None] * bands[None, :]).astype(np.float32)
    z = jnp.concatenate([jnp.asarray(t)[:, None], jnp.cos(jnp.asarray(fw)), -jnp.sin(jnp.asarray(fw))], axis=-1)
    z = jnp.pad(z, ((0, 0), (0, LANES - HY_EMB)))
    padh = LANES - HY_FFN
    w1p = jnp.pad(w1, ((0, LANES - HY_EMB), (0, padh)))
    w2p = jnp.pad(w2, ((0, padh), (0, padh)))
    w3p = jnp.pad(w3, ((0, padh), (0, 0)))
    row = lambda v: jnp.pad(v, (0, padh)).reshape(1, LANES)
    n_out = w3.shape[1]
    tl = min(seq_len, 256)
    full = lambda shape: pl.BlockSpec(shape, lambda i: (0, 0))
    return pl.pallas_call(
        _hy_filter_kernel,
        grid=(seq_len // tl,),
        in_specs=[pl.BlockSpec((tl, LANES), lambda i: (i, 0)), full((LANES, LANES)), full((1, LANES)),
                  full((1, LANES)), full((LANES, LANES)), full((1, LANES)), full((LANES, n_out)), full((1, n_out))],
        out_specs=pl.BlockSpec((tl, n_out), lambda i: (i, 0)),
        out_shape=jax.ShapeDtypeStruct((seq_len, n_out), F32),
        compiler_params=_cparams(("arbitrary",)),
        name="hy_filter",
    )(z, w1p, row(b1), row(freq), w2p, row(b2), w3p, decay.reshape(1, n_out))


@functools.lru_cache(maxsize=None)
def _hy_constants(seq_len):
    nfft = 2 * seq_len
    n2, jj, g = HY_N2, HY_J, HY_G
    n1 = nfft // n2
    h1 = n1 // 2
    k1 = np.arange(n1)

    def kron_mat(rows_n, cols_n, sign, both_parts_in):
        m = np.zeros((rows_n, 2, jj, cols_n, 2 if both_parts_in else 1, jj))
        for j in range(jj):
            r = np.arange(rows_n)[:, None]
            c = np.arange(cols_n)[None, :]
            kk = r if sign < 0 else c
            ph = sign * 2.0 * np.pi * (r * c / n1 + kk * j / nfft)
            cr, ci = np.cos(ph), np.sin(ph)
            m[:, 0, j, :, 0, j] = cr
            m[:, 1, j, :, 0, j] = ci
            if both_parts_in:
                m[:, 0, j, :, 1, j] = -ci
                m[:, 1, j, :, 1, j] = cr
        return m.reshape(rows_n * 2 * jj, -1)

    ma = kron_mat(n1, h1, -1.0, True)
    mak = kron_mat(n1, n1, -1.0, False)
    mai = kron_mat(h1, n1, +1.0, True)

    nn = (np.arange(g)[:, None] * jj + np.arange(jj)[None, :])
    kk2 = np.arange(n2)
    ph = -2.0 * np.pi * kk2[:, None, None] * nn[None] / n2
    mc = np.zeros((2, n2, g, 2, jj))
    mc[0, :, :, 0, :] = np.cos(ph)
    mc[0, :, :, 1, :] = -np.sin(ph)
    mc[1, :, :, 0, :] = np.sin(ph)
    mc[1, :, :, 1, :] = np.cos(ph)
    mc = mc.reshape(2 * n2, 2 * n2)
    mci = np.zeros((g, 2, jj, 2, n2))
    phi = np.transpose(-ph, (1, 2, 0))
    mci[:, 0, :, 0, :] = np.cos(phi)
    mci[:, 0, :, 1, :] = -np.sin(phi)
    mci[:, 1, :, 0, :] = np.sin(phi)
    mci[:, 1, :, 1, :] = np.cos(phi)
    mci = mci.reshape(2 * n2, 2 * n2)

    th = 2.0 * np.pi * (np.arange(g)[:, None] * jj) * k1[None, :] / nfft
    twc = np.cos(th).astype(np.float32).reshape(-1)
    tws = (-np.sin(th)).astype(np.float32).reshape(-1)
    to_bf = lambda m: m.astype(np.float32).astype(BF16)
    return dict(ma=to_bf(ma), mak=to_bf(mak), mai=to_bf(mai), mc=to_bf(mc), mci=to_bf(mci), twc=twc, tws=tws,
                n1=n1, h1=h1)


def _hy_twiddle_store(y, g, twc_ref, tws_ref, a_ref, n1, conj):
    off = g * HY_TILE
    for k1 in range(n1):
        c = twc_ref[g * n1 + k1]
        s = tws_ref[g * n1 + k1]
        if conj:
            s = -s
        yr = y[HY_TILE * k1:HY_TILE * k1 + HY_J]
        yi = y[HY_TILE * k1 + HY_J:HY_TILE * (k1 + 1)]
        t = jnp.concatenate([c * yr - s * yi, s * yr + c * yi], axis=0).astype(BF16)
        a_ref[pl.ds(pl.multiple_of(k1 * 2 * HY_N2 + off, HY_TILE), HY_TILE), :] = t


def _hy_khat_kernel(twc_ref, tws_ref, kern_ref, mak_ref, mc_ref, kh_ref, a_ref, *, n1):
    nfft = n1 * HY_N2

    def stage_a(g, carry):
        slabs = [kern_ref[pl.ds(pl.multiple_of(r * HY_N2 + g * HY_J, HY_J), HY_J), :] for r in range(n1)]
        rhs = jnp.concatenate(slabs, axis=0).astype(BF16)
        y = jnp.dot(mak_ref[...], rhs, preferred_element_type=F32)
        _hy_twiddle_store(y, g, twc_ref, tws_ref, a_ref, n1, False)
        return carry

    lax.fori_loop(0, HY_G, stage_a, 0)

    def stage_c(k1, carry):
        blk = pl.ds(pl.multiple_of(k1 * 2 * HY_N2, 2 * HY_N2), 2 * HY_N2)
        x = jnp.dot(mc_ref[...], a_ref[blk, :], preferred_element_type=F32)
        kh_ref[blk, :] = (x * (1.0 / nfft)).astype(BF16)
        return carry

    lax.fori_loop(0, n1, stage_c, 0)


def _hy_khat_call(kern, consts):
    nfft, c = kern.shape
    n1 = consts["n1"]
    rows = n1 * 2 * HY_N2
    return pl.pallas_call(
        functools.partial(_hy_khat_kernel, n1=n1),
        grid=(c // HY_CB,),
        in_specs=[pl.BlockSpec(memory_space=pltpu.SMEM), pl.BlockSpec(memory_space=pltpu.SMEM),
                  pl.BlockSpec((nfft, HY_CB), lambda i: (0, i)),
                  pl.BlockSpec(consts["mak"].shape, lambda i: (0, 0)),
                  pl.BlockSpec(consts["mc"].shape, lambda i: (0, 0))],
        out_specs=pl.BlockSpec((rows, HY_CB), lambda i: (0, i)),
        out_shape=jax.ShapeDtypeStruct((rows, c), BF16),
        scratch_shapes=[pltpu.VMEM((rows, HY_CB), BF16)],
        compiler_params=_cparams(("arbitrary",)),
        name="hy_khat",
    )(jnp.asarray(consts["twc"]), jnp.asarray(consts["tws"]), kern, jnp.asarray(consts["mak"]),
      jnp.asarray(consts["mc"]))


def _hy_conv3_chunk(ref, base, w, b, r, n_chunks, seq_len):
    r0 = pl.multiple_of(r * HY_N2, HY_N2)
    cur = ref[pl.ds(base + r0, HY_N2), :].astype(F32)
    pr0 = pl.multiple_of(jnp.maximum(r0 - HY_TILE, 0), HY_TILE)
    nx0 = pl.multiple_of(jnp.minimum(r0 + HY_N2, seq_len - HY_TILE), HY_TILE)
    prev_row = ref[pl.ds(base + pr0, HY_TILE), :].astype(F32)[HY_TILE - 1:HY_TILE]
    next_row = ref[pl.ds(base + nx0, HY_TILE), :].astype(F32)[0:1]
    prev_row = jnp.where(r > 0, prev_row, 0.0)
    next_row = jnp.where(r < n_chunks - 1, next_row, 0.0)
    row = lax.broadcasted_iota(jnp.int32, cur.shape, 0)
    up = jnp.where(row == 0, prev_row, pltpu.roll(cur, 1, 0))
    dn = jnp.where(row == HY_N2 - 1, next_row, pltpu.roll(cur, HY_N2 - 1, 0))
    return up * w[0:1] + cur * w[1:2] + dn * w[2:3] + b


def _hy_interleave(a, b):
    parts = []
    for g in range(HY_G):
        parts.append(a[g * HY_J:(g + 1) * HY_J])
        parts.append(b[g * HY_J:(g + 1) * HY_J])
    return jnp.concatenate(parts, axis=0)


def _hy_conv_kernel(twc_ref, tws_ref, x0_ref, x1_ref, v_ref,
                    cw0_ref, cw1_ref, cw2_ref, cb0_ref, cb1_ref, cb2_ref, bias_ref, kh_ref,
                    ma_ref, mc_ref, mci_ref, mai_ref, o_ref, zc_ref, x0s_ref, a_ref, *, n1, seq_len):
    h1 = n1 // 2
    tile_rows = 2 * HY_N2

    def prologue(r, carry):
        cw0, cw1, cw2 = cw0_ref[...], cw1_ref[...], cw2_ref[...]
        cb0, cb1, cb2 = cb0_ref[...], cb1_ref[...], cb2_ref[...]
        zs, x0s = [], []
        for base in (0, seq_len):
            x0s.append(_hy_conv3_chunk(x0_ref, base, cw0, cb0, r, h1, seq_len))
            x1c = _hy_conv3_chunk(x1_ref, base, cw1, cb1, r, h1, seq_len)
            vc = _hy_conv3_chunk(v_ref, base, cw2, cb2, r, h1, seq_len)
            zs.append(x1c * vc)
        dst = pl.ds(pl.multiple_of(r * tile_rows, tile_rows), tile_rows)
        zc_ref[dst, :] = _hy_interleave(zs[0], zs[1]).astype(BF16)
        x0s_ref[dst, :] = _hy_interleave(x0s[0], x0s[1]).astype(BF16)
        return carry

    lax.fori_loop(0, h1, prologue, 0)

    def stage_a(g, carry):
        off = g * HY_TILE
        tiles = [zc_ref[pl.ds(pl.multiple_of(r * tile_rows + off, HY_TILE), HY_TILE), :] for r in range(h1)]
        y = jnp.dot(ma_ref[...], jnp.concatenate(tiles, axis=0), preferred_element_type=F32)
        _hy_twiddle_store(y, g, twc_ref, tws_ref, a_ref, n1, False)
        return carry

    lax.fori_loop(0, HY_G, stage_a, 0)

    def stage_c(k1, carry):
        blk = pl.ds(pl.multiple_of(k1 * tile_rows, tile_rows), tile_rows)
        x = jnp.dot(mc_ref[...], a_ref[blk, :], preferred_element_type=F32)
        kh = kh_ref[blk, :].astype(F32)
        xr, xi = x[:HY_N2], x[HY_N2:]
        kr, ki = kh[:HY_N2], kh[HY_N2:]
        pm = jnp.concatenate([xr * kr - xi * ki, xr * ki + xi * kr], axis=0).astype(BF16)
        a_ref[blk, :] = jnp.dot(mci_ref[...], pm, preferred_element_type=F32).astype(BF16)
        return carry

    lax.fori_loop(0, n1, stage_c, 0)

    bias = bias_ref[...]

    def stage_ai(g, carry):
        off = g * HY_TILE
        tiles = []
        for k1 in range(n1):
            t = a_ref[pl.ds(pl.multiple_of(k1 * tile_rows + off, HY_TILE), HY_TILE), :].astype(F32)
            c = twc_ref[g * n1 + k1]
            s = -tws_ref[g * n1 + k1]
            tr, ti = t[:HY_J], t[HY_J:]
            tiles.append(jnp.concatenate([c * tr - s * ti, s * tr + c * ti], axis=0).astype(BF16))
        y = jnp.dot(mai_ref[...], jnp.concatenate(tiles, axis=0), preferred_element_type=F32)
        for r in range(h1):
            src = pl.ds(pl.multiple_of(r * tile_rows + off, HY_TILE), HY_TILE)
            z = zc_ref[src, :].astype(F32)
            x0 = x0s_ref[src, :].astype(F32)
            o = x0 * (y[HY_TILE * r:HY_TILE * (r + 1)] + bias * z)
            dst = pl.ds(pl.multiple_of(r * HY_N2 + g * HY_J, HY_J), HY_J)
            o_ref[dst, :] = o[:HY_J]
            dst2 = pl.ds(pl.multiple_of(seq_len + r * HY_N2 + g * HY_J, HY_J), HY_J)
            o_ref[dst2, :] = o[HY_J:]
        return carry

    lax.fori_loop(0, HY_G, stage_ai, 0)


def _hy_conv_call(p, khat, conv_w, conv_b, bias, consts, *, n_batch, seq_len, row0):
    c = HY_WIDTH
    n1 = consts["n1"]
    h1 = consts["h1"]
    rb0 = row0 // (2 * seq_len)
    ncb = c // HY_CB
    tile_rows = 2 * HY_N2

    def pin(col):
        return pl.BlockSpec((2 * seq_len, HY_CB), lambda cb, pi: (rb0 + pi, col // HY_CB + cb))

    def cws(stream):
        return pl.BlockSpec((3, HY_CB), lambda cb, pi: (0, stream * ncb + cb))

    def cbs(stream):
        return pl.BlockSpec((1, HY_CB), lambda cb, pi: (0, stream * ncb + cb))

    const = lambda a: pl.BlockSpec(a.shape, lambda cb, pi: (0, 0))
    mats = [jnp.asarray(consts[k]) for k in ("ma", "mc", "mci", "mai")]
    in_specs = [pl.BlockSpec(memory_space=pltpu.SMEM), pl.BlockSpec(memory_space=pltpu.SMEM),
                pin(COL_HY_X0), pin(COL_HY_X1), pin(COL_HY_V), cws(0), cws(1), cws(2), cbs(0), cbs(1), cbs(2),
                pl.BlockSpec((1, HY_CB), lambda cb, pi: (0, cb)),
                pl.BlockSpec((n1 * tile_rows, HY_CB), lambda cb, pi: (0, cb))] + [const(m) for m in mats]
    return pl.pallas_call(
        functools.partial(_hy_conv_kernel, n1=n1, seq_len=seq_len),
        grid=(ncb, n_batch // 2),
        in_specs=in_specs,
        out_specs=pl.BlockSpec((2 * seq_len, HY_CB), lambda cb, pi: (pi, cb)),
        out_shape=jax.ShapeDtypeStruct((n_batch * seq_len, c), F32),
        scratch_shapes=[pltpu.VMEM((h1 * tile_rows, HY_CB), BF16), pltpu.VMEM((h1 * tile_rows, HY_CB), BF16),
                        pltpu.VMEM((n1 * tile_rows, HY_CB), BF16)],
        compiler_params=_cparams(("arbitrary", "arbitrary"), 56),
        name="hy_conv",
    )(jnp.asarray(consts["twc"]), jnp.asarray(consts["tws"]), p, p, p,
      conv_w, conv_w, conv_w, conv_b.reshape(1, -1), conv_b.reshape(1, -1), conv_b.reshape(1, -1),
      bias.reshape(1, -1), khat, *mats)


def _hyena_layer(p, params, *, n_batch, seq_len, row0):
    conv_w, conv_b, w1, b1, freq, w2, b2, w3, decay, bias = params
    consts = _hy_constants(seq_len)
    h = _hy_filter_call(seq_len, w1, b1, freq, w2, b2, w3, decay)
    h_fwd, h_bwd = h[:, :HY_WIDTH], h[:, HY_WIDTH:]
    kern = jnp.concatenate([h_fwd, jnp.zeros((1, HY_WIDTH), F32), h_bwd[:0:-1]], axis=0)
    khat = _hy_khat_call(kern, consts)
    return _hy_conv_call(p, khat, conv_w, conv_b, bias, consts, n_batch=n_batch, seq_len=seq_len, row0=row0)


ROW_WORDS = D_MODEL // 2
ROW_SUB = ROW_WORDS // LANES


def _pack_rows(val, ref2d, n_rows):
    lo = lax.bitcast_convert_type(val[:, :ROW_WORDS].astype(BF16).astype(F32), jnp.uint32)
    hi = lax.bitcast_convert_type(val[:, ROW_WORDS:].astype(BF16).astype(F32), jnp.uint32)
    word = (hi & jnp.uint32(0xFFFF0000)) | (lo >> 16)
    for s in range(ROW_SUB):
        ref2d[pl.ds(s, n_rows, stride=ROW_SUB), :] = word[:, s * LANES:(s + 1) * LANES]


def _unpack_words(w):
    lo = lax.bitcast_convert_type(w << 16, F32)
    hi = lax.bitcast_convert_type(w & jnp.uint32(0xFFFF0000), F32)
    return lo, hi


def _merge_kernel(*refs, has_ctx, n_lat_tiles, tm):
    if has_ctx:
        (x_ref, g1_ref, sh2_ref, sc2_ref, nw2_ref, gl0_ref, gl1_ref, gl2_ref, ya_l, yb_l, yc_l, ya_c, yb_c, yc_c,
         wb_ref, wo_ref, rw_ref, rb_ref, xo_ref, h2p_ref, lg_ref) = refs
    else:
        (x_ref, g1_ref, sh2_ref, sc2_ref, nw2_ref, gl0_ref, gl1_ref, gl2_ref, ya_l, yb_l, yc_l,
         wb_ref, wo_ref, rw_ref, rb_ref, xo_ref, h2p_ref, lg_ref) = refs
    ys = [ya_l[...].astype(BF16), yb_l[...], yc_l[...]]
    if has_ctx:
        is_ctx = pl.program_id(0) >= n_lat_tiles
        ys = [jnp.where(is_ctx, c, l) for c, l in zip((ya_c[...].astype(BF16), yb_c[...], yc_c[...]), ys)]
    m = None
    for g, gl_ref in enumerate((gl0_ref, gl1_ref, gl2_ref)):
        proj = jnp.dot(ys[g], wb_ref[g], preferred_element_type=F32)
        term = jax.nn.sigmoid(gl_ref[...].astype(F32)) * proj
        m = term if m is None else m + term
    out = jnp.dot(m.astype(BF16), wo_ref[...], preferred_element_type=F32)
    xn = x_ref[...] + g1_ref[...] * out
    xo_ref[...] = xn
    h2 = _rms_modulate(xn, nw2_ref[...], sh2_ref[...], 1.0 + sc2_ref[...])
    lg_ref[...] = _dot3(h2, rw_ref[...]) + rb_ref[...]
    _pack_rows(h2, h2p_ref, tm)


def _merge_call(xf, p, mods_l, norm2_w, ys_lat, ys_ctx, wb_bf, wo_bf, rw_pad, rb_pad, *, n_batch, n_seq, n_rows):
    d = D_MODEL
    has_ctx = ys_ctx is not None
    tm = 256
    tpb = n_seq // tm
    nlt = n_batch * tpb
    nt = n_rows // tm

    def mod_spec(comp):
        return pl.BlockSpec((None, None, 1, d), lambda i: (_mod_row(i, tpb, nlt, n_batch), comp, 0, 0))

    def gate_spec(g):
        return pl.BlockSpec((tm, d), lambda i: (i, COL_GATES // d + g))

    lat_spec = pl.BlockSpec((tm, MIX_WIDTH), lambda i: (jnp.minimum(i, nlt - 1), 0))
    ctx_spec = pl.BlockSpec((tm, MIX_WIDTH), lambda i: (jnp.maximum(i - nlt, 0), 0))
    once = pl.Buffered(1)
    in_specs = [pl.BlockSpec((tm, d), lambda i: (i, 0)), mod_spec(2), mod_spec(3), mod_spec(4),
                pl.BlockSpec((1, d), lambda i: (0, 0)), gate_spec(0), gate_spec(1), gate_spec(2),
                lat_spec, lat_spec, lat_spec]
    args = [xf, mods_l, mods_l, mods_l, norm2_w.reshape(1, d), p, p, p, *ys_lat]
    if has_ctx:
        in_specs += [ctx_spec, ctx_spec, ctx_spec]
        args += list(ys_ctx)
    in_specs += [pl.BlockSpec((N_BRANCH, MIX_WIDTH, d), lambda i: (0, 0, 0), pipeline_mode=once),
                 pl.BlockSpec((d, d), lambda i: (0, 0), pipeline_mode=once),
                 pl.BlockSpec((d, LANES), lambda i: (0, 0)), pl.BlockSpec((1, LANES), lambda i: (0, 0))]
    args += [wb_bf, wo_bf, rw_pad, rb_pad]
    return pl.pallas_call(
        functools.partial(_merge_kernel, has_ctx=has_ctx, n_lat_tiles=nlt, tm=tm),
        grid=(nt,),
        in_specs=in_specs,
        out_specs=[pl.BlockSpec((tm, d), lambda i: (i, 0)),
                   pl.BlockSpec((tm * ROW_SUB, LANES), lambda i: (i, 0)),
                   pl.BlockSpec((tm, LANES), lambda i: (i, 0))],
        out_shape=[jax.ShapeDtypeStruct((n_rows, d), F32),
                   jax.ShapeDtypeStruct((n_rows * ROW_SUB, LANES), jnp.uint32),
                   jax.ShapeDtypeStruct((n_rows, LANES), F32)],
        compiler_params=_cparams(("arbitrary",), 58),
        name="merge",
    )(*args)


def _route_kernel(lg_ref, idx_ref, rank_ref, gate_ref, cnt_ref, carry_ref, *, tr):
    i = pl.program_id(0)

    @pl.when(i == 0)
    def _():
        carry_ref[...] = jnp.zeros_like(carry_ref)

    lane = lax.broadcasted_iota(jnp.int32, (tr, LANES), 1)
    lg = jnp.where(lane < N_EXPERTS, lg_ref[...], -jnp.inf)
    tops, hots = [], []
    for _ in range(TOP_K):
        mx = jnp.max(lg, axis=-1, keepdims=True)
        sel = jnp.min(jnp.where(lg == mx, lane, LANES), axis=-1, keepdims=True)
        hot = lane == sel
        tops.append((mx, sel))
        hots.append(hot)
        lg = jnp.where(hot, -jnp.inf, lg)
    es = [jnp.exp(mx - tops[0][0]) for mx, _ in tops]
    den = es[0] + es[1] + es[2] + es[3]
    hot_f = [jnp.where(h, 1.0, 0.0) for h in hots]
    chosen = hot_f[0] + hot_f[1] + hot_f[2] + hot_f[3]
    r_i = lax.broadcasted_iota(jnp.int32, (tr, tr), 0)
    c_i = lax.broadcasted_iota(jnp.int32, (tr, tr), 1)
    below = jnp.where(c_i < r_i, 1.0, 0.0).astype(BF16)
    before = jnp.dot(below, chosen.astype(BF16), preferred_element_type=F32) + carry_ref[...]
    idx_o = jnp.zeros((tr, LANES), jnp.int32)
    rank_o = jnp.zeros((tr, LANES), jnp.int32)
    gate_o = jnp.zeros((tr, LANES), F32)
    for j in range(TOP_K):
        rank_j = jnp.sum(hot_f[j] * before, axis=-1, keepdims=True).astype(jnp.int32)
        idx_o = jnp.where(lane == j, tops[j][1], idx_o)
        rank_o = jnp.where(lane == j, rank_j, rank_o)
        gate_o = jnp.where(lane == j, es[j] / den, gate_o)
    idx_ref[...] = idx_o
    rank_ref[...] = rank_o
    gate_ref[...] = gate_o
    carry_ref[...] += jnp.sum(chosen, axis=0, keepdims=True)
    cnt_ref[...] = carry_ref[...]


def _route_call(logits):
    t_rows = logits.shape[0]
    tr = 512
    tile = pl.BlockSpec((tr, LANES), lambda i: (i, 0))
    return pl.pallas_call(
        functools.partial(_route_kernel, tr=tr),
        grid=(t_rows // tr,),
        in_specs=[tile],
        out_specs=[tile, tile, tile, pl.BlockSpec((1, LANES), lambda i: (0, 0))],
        out_shape=[jax.ShapeDtypeStruct((t_rows, LANES), jnp.int32), jax.ShapeDtypeStruct((t_rows, LANES), jnp.int32),
                   jax.ShapeDtypeStruct((t_rows, LANES), F32), jax.ShapeDtypeStruct((1, LANES), F32)],
        scratch_shapes=[pltpu.VMEM((1, LANES), F32)],
        compiler_params=_cparams(("arbitrary",)),
        name="route",
    )(logits)


MOE_BLK = 256


def _row_tile(ref2d, row):
    return ref2d.at[pl.ds(pl.multiple_of(row * ROW_SUB, ROW_SUB), ROW_SUB), :]


def _dispatch_kernel(dest_ref, cnt_ref, pad_ref, off_ref, h_ref, xs_ref, zero_ref, sem, zsem, *, td):
    i = pl.program_id(0)

    @pl.when(i == 0)
    def _():
        zero_ref[...] = jnp.zeros_like(zero_ref)
        for e in range(N_EXPERTS):
            base = off_ref[e]

            def fill(r, carry):
                pltpu.make_async_copy(zero_ref, _row_tile(xs_ref, base + r), zsem).start()
                return carry

            def drain(r, carry):
                pltpu.make_async_copy(zero_ref, _row_tile(xs_ref, base + r), zsem).wait()
                return carry

            lax.fori_loop(cnt_ref[e], pad_ref[e], fill, 0)
            lax.fori_loop(cnt_ref[e], pad_ref[e], drain, 0)

    def issue(r, carry):
        src = _row_tile(h_ref, r)
        for j in range(TOP_K):
            pltpu.make_async_copy(src, _row_tile(xs_ref, dest_ref[r * TOP_K + j]), sem).start()
        return carry

    lax.fori_loop(0, td, issue, 0)

    def drain_all(r, carry):
        src = _row_tile(h_ref, r)
        for j in range(TOP_K):
            pltpu.make_async_copy(src, _row_tile(xs_ref, dest_ref[r * TOP_K + j]), sem).wait()
        return carry

    lax.fori_loop(0, td, drain_all, 0)


def _dispatch_call(h2p, dest_flat, counts, padded, offsets, n_rows_sorted):
    t_rows = h2p.shape[0] // ROW_SUB
    td = 256
    smem = pl.BlockSpec(memory_space=pltpu.SMEM)
    return pl.pallas_call(
        functools.partial(_dispatch_kernel, td=td),
        grid=(t_rows // td,),
        in_specs=[pl.BlockSpec((td * TOP_K,), lambda i: (i,), memory_space=pltpu.SMEM), smem, smem, smem,
                  pl.BlockSpec((td * ROW_SUB, LANES), lambda i: (i, 0))],
        out_specs=pl.BlockSpec(memory_space=pl.ANY),
        out_shape=jax.ShapeDtypeStruct((n_rows_sorted * ROW_SUB, LANES), jnp.uint32),
        scratch_shapes=[pltpu.VMEM((ROW_SUB, LANES), jnp.uint32), pltpu.SemaphoreType.DMA(()),
                        pltpu.SemaphoreType.DMA(())],
        compiler_params=_cparams(("arbitrary",)),
        name="moe_dispatch",
    )(dest_flat, counts, padded, offsets, h2p)


def _expert_kernel(be_ref, na_ref, x_ref, w1_ref, b1_ref, w2_ref, b2_ref, y_ref, w1s_ref, w2s_ref):
    b = pl.program_id(0)
    prev = be_ref[jnp.maximum(b - 1, 0)]
    fresh = (b == 0) | (be_ref[b] != prev)

    @pl.when(fresh & (b < na_ref[0]))
    def _():
        for s in range(ROW_SUB):
            w1s_ref[s, :LANES, :] = w1_ref[s * LANES:(s + 1) * LANES, :].astype(BF16)
            w1s_ref[s, LANES:, :] = w1_ref[ROW_WORDS + s * LANES:ROW_WORDS + (s + 1) * LANES, :].astype(BF16)
        w2s_ref[...] = w2_ref[...].astype(BF16)

    @pl.when(b < na_ref[0])
    def _():
        acc = None
        for s in range(ROW_SUB):
            lo, hi = _unpack_words(x_ref[pl.ds(s, MOE_BLK, stride=ROW_SUB), :])
            xc = jnp.concatenate([lo.astype(BF16), hi.astype(BF16)], axis=1)
            part = jnp.dot(xc, w1s_ref[s], preferred_element_type=F32)
            acc = part if acc is None else acc + part
        a = acc + b1_ref[...]
        a_glu = jnp.minimum(a[:, :EXPERT_FF], SWIGLU_LIMIT)
        a_lin = jnp.clip(a[:, EXPERT_FF:], -SWIGLU_LIMIT, SWIGLU_LIMIT)
        act = a_glu * jax.nn.sigmoid(SWIGLU_ALPHA * a_glu) * (a_lin + 1.0)
        y = jnp.dot(act.astype(BF16), w2s_ref[...], preferred_element_type=F32) + b2_ref[...]
        _pack_rows(y, y_ref, MOE_BLK)


def _expert_call(xs, block_e, n_active, w1, b1, w2, b2, layer):
    n_blocks = block_e.shape[0]
    depth, n_exp, d, ff2 = w1.shape
    ff = w2.shape[2]

    def rows(b, be, na):
        return (jnp.minimum(b, na[0] - 1), 0)

    def wspec(r, c):
        return pl.BlockSpec((None, None, r, c), lambda b, be, na: (layer, be[b], 0, 0))

    return pl.pallas_call(
        _expert_kernel,
        grid_spec=pltpu.PrefetchScalarGridSpec(
            num_scalar_prefetch=2,
            grid=(n_blocks,),
            in_specs=[pl.BlockSpec((MOE_BLK * ROW_SUB, LANES), rows),
                      wspec(d, ff2), wspec(1, ff2), wspec(ff, d), wspec(1, d)],
            out_specs=pl.BlockSpec((MOE_BLK * ROW_SUB, LANES), rows),
            scratch_shapes=[pltpu.VMEM((ROW_SUB, 2 * LANES, ff2), BF16), pltpu.VMEM((ff, d), BF16)],
        ),
        out_shape=jax.ShapeDtypeStruct(xs.shape, jnp.uint32),
        compiler_params=_cparams(("arbitrary",), 56),
        name="moe_experts",
    )(block_e, n_active, xs, w1, b1.reshape(depth, n_exp, 1, ff2), w2, b2.reshape(depth, n_exp, 1, d))


def _combine_kernel(dest_ref, gate_ref, x_ref, g2_ref, fw_ref, ys_ref, o_ref, buf_ref, sem, *, tc, final_norm):
    def issue(r, carry):
        for j in range(TOP_K):
            pltpu.make_async_copy(_row_tile(ys_ref, dest_ref[r * TOP_K + j]), _row_tile(buf_ref, j * tc + r),
                                  sem).start()
        return carry

    lax.fori_loop(0, tc, issue, 0)

    def drain(r, carry):
        for j in range(TOP_K):
            pltpu.make_async_copy(_row_tile(ys_ref, dest_ref[r * TOP_K + j]), _row_tile(buf_ref, j * tc + r),
                                  sem).wait()
        return carry

    lax.fori_loop(0, tc, drain, 0)

    gate = gate_ref[...]
    gs = [gate[:, j:j + 1] for j in range(TOP_K)]
    lo_parts, hi_parts = [], []
    for s in range(ROW_SUB):
        f_lo = None
        f_hi = None
        for j in range(TOP_K):
            lo, hi = _unpack_words(buf_ref[pl.ds(j * tc * ROW_SUB + s, tc, stride=ROW_SUB), :])
            f_lo = gs[j] * lo if f_lo is None else f_lo + gs[j] * lo
            f_hi = gs[j] * hi if f_hi is None else f_hi + gs[j] * hi
        lo_parts.append(f_lo)
        hi_parts.append(f_hi)
    f = jnp.concatenate(lo_parts + hi_parts, axis=1)
    xn = x_ref[...] + g2_ref[...] * f
    if final_norm:
        ms = jnp.mean(xn * xn, axis=-1, keepdims=True)
        xn = xn * lax.rsqrt(ms + NORM_EPS) * fw_ref[...]
    o_ref[...] = xn


def _combine_call(xf, ys, dest_flat, gates, mods_l, final_w, *, n_batch, n_seq, final_norm):
    t_rows, d = xf.shape
    tc = 256
    tpb = n_seq // tc
    nlt = n_batch * tpb
    return pl.pallas_call(
        functools.partial(_combine_kernel, tc=tc, final_norm=final_norm),
        grid=(t_rows // tc,),
        in_specs=[pl.BlockSpec((tc * TOP_K,), lambda i: (i,), memory_space=pltpu.SMEM),
                  pl.BlockSpec((tc, LANES), lambda i: (i, 0)),
                  pl.BlockSpec((tc, d), lambda i: (i, 0)),
                  pl.BlockSpec((None, None, 1, d), lambda i: (_mod_row(i, tpb, nlt, n_batch), 5, 0, 0)),
                  pl.BlockSpec((1, d), lambda i: (0, 0)),
                  pl.BlockSpec(memory_space=pl.ANY)],
        out_specs=pl.BlockSpec((tc, d), lambda i: (i, 0)),
        out_shape=jax.ShapeDtypeStruct((t_rows, d), F32),
        scratch_shapes=[pltpu.VMEM((TOP_K * tc * ROW_SUB, LANES), jnp.uint32), pltpu.SemaphoreType.DMA(())],
        compiler_params=_cparams(("arbitrary",)),
        name="moe_combine",
    )(dest_flat, gates, xf, mods_l, final_w.reshape(1, d), ys)


def _moe_layer(xf, h2p, logits, mods_l, w1, b1, w2, b2, layer, final_w, *, n_batch, n_seq, final_norm):
    t_rows = xf.shape[0]
    idx, rank, gates, cnt = _route_call(logits)
    counts = cnt[0, :N_EXPERTS].astype(jnp.int32)
    padded = (counts + MOE_BLK - 1) // MOE_BLK * MOE_BLK
    ends = jnp.cumsum(padded)
    offsets = ends - padded
    dest = (offsets[idx[:, :TOP_K]] + rank[:, :TOP_K]).reshape(-1)
    n_blocks = (t_rows * TOP_K + N_EXPERTS * (MOE_BLK - 1)) // MOE_BLK
    block_e = jnp.minimum(jnp.searchsorted(ends, jnp.arange(n_blocks, dtype=jnp.int32) * MOE_BLK, side='right'),
                          N_EXPERTS - 1).astype(jnp.int32)
    n_active = (ends[-1:] // MOE_BLK).astype(jnp.int32)
    xs = _dispatch_call(h2p, dest, counts, padded, offsets, n_blocks * MOE_BLK)
    ys = _expert_call(xs, block_e, n_active, w1, b1, w2, b2, layer)
    return _combine_call(xf, ys, dest, gates, mods_l, final_w, n_batch=n_batch, n_seq=n_seq, final_norm=final_norm)


def _permute_w_in(w_in, layer):
    pad = jnp.zeros((w_in.shape[1], P_PAD - w_in.shape[2]), w_in.dtype)
    pieces = [w_in[layer, :, lo:hi] for lo, hi in
              ((0, 4096), (5376, 7424), (7456, 13600), (4352, 5376), (4096, 4352), (7424, 7456))]
    return jnp.concatenate(pieces + [pad], axis=-1).astype(BF16)


def kernel(x, c, ctx, c_ctx, ada_w, ada_b, norm1_w, norm2_w, w_in, hy_conv_w, hy_conv_b, hy_filt_w1, hy_filt_b1,
           hy_filt_freq, hy_filt_w2, hy_filt_b2, hy_filt_w3, hy_decay, hy_bias, attn_sink, gla_a_w, gla_a_b,
           gla_norm_w, w_branch, w_out, router_w, router_b, moe_w1, moe_b1, moe_w2, moe_b2, final_norm_w):
    n_batch, n_seq, d = x.shape
    n_ctx = ctx.shape[1]
    depth = ada_w.shape[0]
    t_lat = n_batch * n_seq
    xf = jnp.concatenate([x.reshape(t_lat, d), ctx.reshape(n_batch * n_ctx, d)], axis=0)
    cond = jnp.zeros((SUBLANES, d), F32).at[:n_batch].set(c).at[n_batch].set(c_ctx)
    mods = _ada_call(cond, ada_w, ada_b).reshape(depth, SUBLANES, N_ADA, 1, d)
    rw_pad = jnp.pad(router_w, ((0, 0), (0, 0), (0, LANES - N_EXPERTS)))
    rb_pad = jnp.pad(router_b, ((0, 0), (0, LANES - N_EXPERTS))).reshape(depth, 1, LANES)
    cos_t, sin_t = _rope_tables(n_seq)
    q_col = COL_ATT_Q // MIX_WIDTH
    v_col = COL_ATT_V // LANES
    for l in range(depth):
        need_ctx = l < depth - 1
        last = l == depth - 1
        mods_l = mods[l]
        p = _in_call(xf, norm1_w[l], mods_l, _permute_w_in(w_in, l), n_batch=n_batch, n_seq=n_seq)
        hy_params = (hy_conv_w[l], hy_conv_b[l], hy_filt_w1[l], hy_filt_b1[l], hy_filt_freq[l], hy_filt_w2[l],
                     hy_filt_b2[l], hy_filt_w3[l], hy_decay[l], hy_bias[l])
        y_a = _hyena_layer(p, hy_params, n_batch=n_batch, seq_len=n_seq, row0=0)
        k_r = _rope_k_call(p, cos_t, sin_t, n_batch=n_batch, n_seq=n_seq)
        y_b = _attn_call(attn_sink[l], p, q_col, 0, k_r, 0, p, v_col, p, cos_t, sin_t, n_batch=n_batch,
                         n_seq=n_seq, n_ctx=n_ctx, ctx_row0=t_lat, has_local=True)
        y_c, y_cc = _gla_layer(p, gla_a_w[l], gla_a_b[l], gla_norm_w[l], n_batch=n_batch, n_seq=n_seq,
                               n_ctx=n_ctx, need_ctx=need_ctx)
        ys_ctx = None
        if need_ctx:
            y_ac = _hyena_layer(p, hy_params, n_batch=n_batch, seq_len=n_ctx, row0=t_lat)
            y_bc = _attn_call(attn_sink[l], p, q_col, t_lat, None, 0, None, 0, p, None, None, n_batch=n_batch,
                              n_seq=n_ctx, n_ctx=n_ctx, ctx_row0=t_lat, has_local=False)
            ys_ctx = (y_ac, y_bc, y_cc)
        n_rows = xf.shape[0] if need_ctx else t_lat
        xf, h2p, logits = _merge_call(xf, p, mods_l, norm2_w[l], (y_a, y_b, y_c), ys_ctx,
                                      w_branch[l].astype(BF16), w_out[l].astype(BF16),
                                      rw_pad[l], rb_pad[l], n_batch=n_batch, n_seq=n_seq, n_rows=n_rows)
        xf = _moe_layer(xf, h2p, logits, mods_l, moe_w1, moe_b1, moe_w2, moe_b2, l, final_norm_w,
                        n_batch=n_batch, n_seq=n_seq, final_norm=last)
    return xf.reshape(n_batch, n_seq, d)
```

```python
import functools
import math

import jax
import jax.numpy as jnp
import numpy as np
from jax import lax
from jax.experimental import pallas as pl
from jax.experimental.pallas import tpu as pltpu

BF16 = jnp.bfloat16
F32 = jnp.float32

D_MODEL = 2048
DEPTH = 4
GRID_W = 64
NORM_EPS = 1e-5
N_ADA = 6
MIX_WIDTH = D_MODEL // 2

HY_WIDTH = MIX_WIDTH
HY_BANDS = 16
HY_EMB = 1 + 2 * HY_BANDS
HY_FFN = 64

ATT_HEAD_DIM = 64
ATT_HEADS = MIX_WIDTH // ATT_HEAD_DIM
ATT_KV_HEADS = 2
ATT_GROUP = ATT_HEADS // ATT_KV_HEADS
WINDOW = 128
ATT_BLOCK = 128
ROPE_BASE = 10000.0
NEG_INF = -1e30

GLA_HEADS = 4
GLA_DV = MIX_WIDTH // GLA_HEADS
GLA_DK = GLA_DV // 2
GLA_RANK = 16
GLA_TAU = 16.0
GLA_CHUNK = 64

N_BRANCH = 3
N_EXPERTS = 32
TOP_K = 4
EXPERT_FF = D_MODEL // 4
SWIGLU_LIMIT = 7.0
SWIGLU_ALPHA = 1.702

LANES = 128
SUBLANES = 8

COL_HY_X0 = 0
COL_HY_X1 = 1024
COL_HY_V = 2048
COL_ATT_Q = 3072
COL_GLA_V = 4096
COL_GLA_R = 5120
COL_GATES = 6144
COL_GLA_Q = 12288
COL_GLA_K = 12800
COL_ATT_K = 13312
COL_ATT_V = 13440
COL_DECAY = 13568
P_PAD = 13824


def _cparams(dims, vmem_mb=48):
    return pltpu.CompilerParams(dimension_semantics=dims, vmem_limit_bytes=vmem_mb << 20)


def _split_bf16(a):
    hi = a.astype(BF16)
    lo = (a - hi.astype(F32)).astype(BF16)
    return hi, lo


def _ada_kernel(c_ref, w_ref, b_ref, o_ref):
    cv = c_ref[...]
    s_hi, s_lo = _split_bf16(cv * jax.nn.sigmoid(cv))
    w_hi, w_lo = _split_bf16(w_ref[...])
    acc = jnp.dot(s_hi, w_hi, preferred_element_type=F32)
    acc += jnp.dot(s_lo, w_hi, preferred_element_type=F32)
    acc += jnp.dot(s_hi, w_lo, preferred_element_type=F32)
    o_ref[...] = acc + b_ref[...]


def _ada_call(cond8, ada_w, ada_b):
    depth, d, n6 = ada_w.shape
    tn = 1024
    return pl.pallas_call(
        _ada_kernel,
        grid=(depth, n6 // tn),
        in_specs=[
            pl.BlockSpec((SUBLANES, d), lambda l, j: (0, 0)),
            pl.BlockSpec((None, d, tn), lambda l, j: (l, 0, j)),
            pl.BlockSpec((None, 1, tn), lambda l, j: (l, 0, j)),
        ],
        out_specs=pl.BlockSpec((None, SUBLANES, tn), lambda l, j: (l, 0, j)),
        out_shape=jax.ShapeDtypeStruct((depth, SUBLANES, n6), F32),
        compiler_params=_cparams(("arbitrary", "arbitrary")),
        name="ada",
    )(cond8, ada_w, ada_b.reshape(depth, 1, n6))


def _rms_modulate(xf, nw, sh, sc1):
    ms = jnp.mean(xf * xf, axis=-1, keepdims=True)
    return (xf * lax.rsqrt(ms + NORM_EPS) * nw) * sc1 + sh


def _in_kernel(x_ref, nw_ref, sh_ref, sc_ref, w_ref, o_ref, h_ref, *, tm, rc):
    @pl.when(pl.program_id(1) == 0)
    def _():
        nw = nw_ref[...]
        sh = sh_ref[...]
        sc1 = 1.0 + sc_ref[...]

        def body(r, carry):
            sl = pl.ds(pl.multiple_of(r * rc, rc), rc)
            h_ref[sl, :] = _rms_modulate(x_ref[sl, :], nw, sh, sc1).astype(BF16)
            return carry

        lax.fori_loop(0, tm // rc, body, 0)

    o_ref[...] = jnp.dot(h_ref[...], w_ref[...], preferred_element_type=F32).astype(o_ref.dtype)


def _token_tile(n_lat_per_batch, n_ctx_total, cap):
    tm = cap
    while n_lat_per_batch % tm or n_ctx_total % tm:
        tm //= 2
    return tm


def _mod_row(i, tiles_per_batch, n_lat_tiles, n_batch):
    return jnp.where(i < n_lat_tiles, i // tiles_per_batch, n_batch)


def _in_call(xf, norm_w, mods_l, w_bf, *, n_batch, n_seq):
    t_rows, d = xf.shape
    p_pad = w_bf.shape[1]
    tm = _token_tile(n_seq, t_rows - n_batch * n_seq, 1024)
    tn = 1536
    tpb = n_seq // tm
    nlt = n_batch * tpb

    def mod_spec(comp):
        return pl.BlockSpec((None, None, 1, d), lambda i, j: (_mod_row(i, tpb, nlt, n_batch), comp, 0, 0))

    return pl.pallas_call(
        functools.partial(_in_kernel, tm=tm, rc=min(tm, 64)),
        grid=(t_rows // tm, p_pad // tn),
        in_specs=[
            pl.BlockSpec((tm, d), lambda i, j: (i, 0)),
            pl.BlockSpec((1, d), lambda i, j: (0, 0)),
            mod_spec(0),
            mod_spec(1),
            pl.BlockSpec((d, tn), lambda i, j: (0, j)),
        ],
        out_specs=pl.BlockSpec((tm, tn), lambda i, j: (i, j)),
        out_shape=jax.ShapeDtypeStruct((t_rows, p_pad), BF16),
        scratch_shapes=[pltpu.VMEM((tm, d), BF16)],
        compiler_params=_cparams(("arbitrary", "arbitrary")),
        name="in_proj",
    )(xf, norm_w.reshape(1, d), mods_l, mods_l, w_bf)


def _rope_tables(n_tokens):
    pos = np.arange(n_tokens)
    row_pos = (pos // GRID_W).astype(np.float32)
    col_pos = (pos % GRID_W).astype(np.float32)
    n_freq = ATT_HEAD_DIM // 4
    inv = (ROPE_BASE ** (-np.arange(n_freq, dtype=np.float32) / n_freq)).astype(np.float32)
    lane = np.arange(LANES)
    dd = lane % ATT_HEAD_DIM
    axis = dd // (2 * n_freq)
    freq = dd % n_freq
    second = (dd % (2 * n_freq)) >= n_freq
    p = jnp.where(jnp.asarray(axis)[None, :] == 0, jnp.asarray(row_pos)[:, None], jnp.asarray(col_pos)[:, None])
    ang = p * jnp.asarray(inv)[jnp.asarray(freq)][None, :]
    sign = jnp.where(jnp.asarray(second), 1.0, -1.0).astype(F32)
    return jnp.cos(ang), jnp.sin(ang) * sign[None, :]


def _rope_tile(xf, cos, sin_signed):
    lane = lax.broadcasted_iota(jnp.int32, xf.shape, 1)
    first = (lane % 32) < 16
    partner = jnp.where(first, pltpu.roll(xf, LANES - 16, 1), pltpu.roll(xf, 16, 1))
    return xf * cos + partner * sin_signed


KV_SLOTS = 4


def _kv_prep_kernel(k_ref, v_ref, cos_ref, sin_ref, o_ref, *, n_lat_tiles):
    kf = k_ref[...].astype(F32)
    kf = jnp.where(pl.program_id(0) < n_lat_tiles, _rope_tile(kf, cos_ref[...], sin_ref[...]), kf)
    vf = v_ref[...].astype(F32)
    half = LANES // 2
    o_ref[:, 0:LANES] = kf.astype(BF16)
    o_ref[:, LANES:2 * LANES] = pltpu.roll(kf, half, 1).astype(BF16)
    o_ref[:, 2 * LANES:3 * LANES] = vf.astype(BF16)
    o_ref[:, 3 * LANES:4 * LANES] = pltpu.roll(vf, half, 1).astype(BF16)


def _kv_prep_call(p, cos_t, sin_t, *, n_batch, n_seq):
    t_rows = p.shape[0]
    tr = _token_tile(n_seq, t_rows - n_batch * n_seq, 256)
    nb = n_seq // tr
    nlt = n_batch * nb
    tab = pl.BlockSpec((tr, LANES), lambda i: (jnp.where(i < nlt, i % nb, 0), 0))
    return pl.pallas_call(
        functools.partial(_kv_prep_kernel, n_lat_tiles=nlt),
        grid=(t_rows // tr,),
        in_specs=[pl.BlockSpec((tr, LANES), lambda i: (i, COL_ATT_K // LANES)),
                  pl.BlockSpec((tr, LANES), lambda i: (i, COL_ATT_V // LANES)), tab, tab],
        out_specs=pl.BlockSpec((tr, KV_SLOTS * LANES), lambda i: (i, 0)),
        out_shape=jax.ShapeDtypeStruct((t_rows, KV_SLOTS * LANES), BF16),
        compiler_params=_cparams(("arbitrary",)),
        name="kv_prep",
    )(p, p, cos_t, sin_t)


def _attn_kernel(*refs, has_local, n_seq):
    if has_local:
        sink_ref, q_ref, kvp_ref, kvc_ref, kvn_ref, kvx_ref, cos_ref, sin_ref, o_ref = refs
    else:
        sink_ref, q_ref, kvx_ref, o_ref = refs
    i = pl.program_id(1)
    npair = ATT_GROUP // 2
    rows = npair * ATT_BLOCK
    n_loc = 3 * ATT_BLOCK if has_local else 0
    if has_local:
        kv_rows = jnp.concatenate([kvp_ref[...], kvc_ref[...], kvn_ref[...], kvx_ref[...]], axis=0)
        r_i = lax.broadcasted_iota(jnp.int32, (rows, n_loc), 0)
        c_i = lax.broadcasted_iota(jnp.int32, (rows, n_loc), 1)
        qpos = i * ATT_BLOCK + (r_i & (ATT_BLOCK - 1))
        kpos = (i - 1) * ATT_BLOCK + c_i
        allowed = (kpos >= 0) & (kpos < n_seq) & (jnp.abs(qpos - kpos) <= WINDOW)
        cos = cos_ref[...]
        sin = sin_ref[...]
    else:
        kv_rows = kvx_ref[...]
    n_keys = kv_rows.shape[0]
    low = lax.broadcasted_iota(jnp.int32, (n_keys, LANES), 1) < ATT_HEAD_DIM
    out_low = lax.broadcasted_iota(jnp.int32, (rows, LANES), 1) < ATT_HEAD_DIM
    pair_of_row = lax.broadcasted_iota(jnp.int32, (rows, 1), 0) // ATT_BLOCK
    zero = jnp.zeros((n_keys, LANES), BF16)
    nt = (((1,), (1,)), ((), ()))

    def slot(s):
        return kv_rows[:, s * LANES:(s + 1) * LANES]

    for g in range(ATT_KV_HEADS):
        ka, kb, va, vb = (slot(0), slot(1), slot(2), slot(3)) if g == 0 else (slot(1), slot(0), slot(3), slot(2))
        k2 = jnp.concatenate([jnp.where(low, ka, zero), jnp.where(low, zero, kb)], axis=0)
        v2 = jnp.concatenate([jnp.where(low, va, zero), jnp.where(low, zero, vb)], axis=0)
        qs = []
        for pr in range(npair):
            hp = g * npair + pr
            qf = q_ref[:, hp * LANES:(hp + 1) * LANES].astype(F32)
            if has_local:
                qf = _rope_tile(qf, cos, sin)
            qs.append((qf * (ATT_HEAD_DIM ** -0.5)).astype(BF16))
        s = lax.dot_general(jnp.concatenate(qs, axis=0), k2, nt, preferred_element_type=F32)
        probs, dens = [], []
        for hf in range(2):
            sink = jnp.zeros((rows, 1), F32)
            for pr in range(npair):
                sink = jnp.where(pair_of_row == pr, sink_ref[(g * npair + pr) * 2 + hf], sink)
            sh = s[:, hf * n_keys:(hf + 1) * n_keys]
            s_x = sh[:, n_loc:]
            m = jnp.maximum(jnp.max(s_x, axis=-1, keepdims=True), sink)
            if has_local:
                s_l = jnp.where(allowed, sh[:, :n_loc], NEG_INF)
                m = jnp.maximum(m, jnp.max(s_l, axis=-1, keepdims=True))
                e_l = jnp.exp(s_l - m)
                probs.append(e_l.astype(BF16))
            e_x = jnp.exp(s_x - m)
            probs.append(e_x.astype(BF16))
            den = jnp.sum(e_x, axis=-1, keepdims=True) + jnp.exp(sink - m)
            if has_local:
                den = den + jnp.sum(e_l, axis=-1, keepdims=True)
            dens.append(den)
        o = jnp.dot(jnp.concatenate(probs, axis=1), v2, preferred_element_type=F32)
        o = o * jnp.where(out_low, 1.0 / dens[0], 1.0 / dens[1])
        for pr in range(npair):
            hp = g * npair + pr
            o_ref[:, hp * LANES:(hp + 1) * LANES] = o[pr * ATT_BLOCK:(pr + 1) * ATT_BLOCK].astype(o_ref.dtype)


def _attn_call(sink, p, kvp, cos_t, sin_t, *, n_batch, n_seq, n_ctx, q_row0, ctx_row0, has_local):
    nq = n_seq // ATT_BLOCK
    qb0 = q_row0 // ATT_BLOCK
    cb0 = ctx_row0 // n_ctx
    kvw = KV_SLOTS * LANES

    def loc(d):
        return pl.BlockSpec((ATT_BLOCK, kvw), lambda b, i: (qb0 + b * nq + jnp.clip(i + d, 0, nq - 1), 0))

    in_specs = [pl.BlockSpec(memory_space=pltpu.SMEM),
                pl.BlockSpec((ATT_BLOCK, MIX_WIDTH), lambda b, i: (qb0 + b * nq + i, COL_ATT_Q // MIX_WIDTH))]
    args = [sink, p]
    if has_local:
        in_specs += [loc(-1), loc(0), loc(1)]
        args += [kvp, kvp, kvp]
    in_specs += [pl.BlockSpec((n_ctx, kvw), lambda b, i: (cb0 + b, 0))]
    args += [kvp]
    if has_local:
        in_specs += [pl.BlockSpec((ATT_BLOCK, LANES), lambda b, i: (i, 0)),
                     pl.BlockSpec((ATT_BLOCK, LANES), lambda b, i: (i, 0))]
        args += [cos_t, sin_t]
    return pl.pallas_call(
        functools.partial(_attn_kernel, has_local=has_local, n_seq=n_seq),
        grid=(n_batch, nq),
        in_specs=in_specs,
        out_specs=pl.BlockSpec((ATT_BLOCK, MIX_WIDTH), lambda b, i: (b * nq + i, 0)),
        out_shape=jax.ShapeDtypeStruct((n_batch * n_seq, MIX_WIDTH), BF16),
        compiler_params=_cparams(("arbitrary", "arbitrary")),
        name="attn_local" if has_local else "attn_ctx",
    )(*args)


def _gla_kernel(*refs, reverse, fuse_out, tb):
    if fuse_out:
        (q_ref, k_ref, v_ref, pa_ref, aw_ref, ab_ref, s0_ref, oo_ref, r_ref, nw_ref,
         y_ref, sout_ref, st_ref, la_ref) = refs
    else:
        q_ref, k_ref, v_ref, pa_ref, aw_ref, ab_ref, s0_ref, y_ref, sout_ref, st_ref, la_ref = refs
    i = pl.program_id(1)
    ck = GLA_CHUNK
    nchunk = tb // ck

    @pl.when(i == 0)
    def _():
        st_ref[...] = s0_ref[...]

    z = jnp.dot(pa_ref[...], aw_ref[...], preferred_element_type=F32) + ab_ref[...]
    la_ref[...] = (jnp.minimum(z, 0.0) - jnp.log(1.0 + jnp.exp(-jnp.abs(z)))) * (1.0 / GLA_TAU)

    r_i = lax.broadcasted_iota(jnp.int32, (ck, ck), 0)
    c_i = lax.broadcasted_iota(jnp.int32, (ck, ck), 1)
    tri = (c_i >= r_i) if reverse else (c_i <= r_i)
    tri_b = jnp.where(tri, 1.0, 0.0).astype(BF16)
    nt = (((1,), (1,)), ((), ()))
    tn = (((0,), (0,)), ((), ()))
    if fuse_out:
        nw = nw_ref[...]

    def chunk(ci, carry):
        c = (nchunk - 1 - ci) if reverse else ci
        rs = pl.ds(pl.multiple_of(c * ck, ck), ck)
        la_hi, la_lo = _split_bf16(la_ref[rs, :])
        b = jnp.dot(tri_b, la_hi, preferred_element_type=F32) + jnp.dot(tri_b, la_lo, preferred_element_type=F32)
        b_tot = b[0:1, :] if reverse else b[ck - 1:ck, :]
        kf = k_ref[rs, :].astype(F32)
        q_in = (q_ref[rs, :].astype(F32) * (GLA_DK ** -0.5) * jnp.exp(b)).astype(BF16)
        k_in = (kf * jnp.exp(-b)).astype(BF16)
        k_st = (kf * jnp.exp(b_tot - b)).astype(BF16)
        dec = jnp.exp(b_tot)
        vv = v_ref[rs, :]
        for h in range(GLA_HEADS):
            hs = slice(h * GLA_DK, (h + 1) * GLA_DK)
            vs = slice(h * GLA_DV, (h + 1) * GLA_DV)
            att = lax.dot_general(q_in[:, hs], k_in[:, hs], nt, preferred_element_type=F32)
            att = jnp.where(tri, att, 0.0).astype(BF16)
            st = st_ref[h]
            o = jnp.dot(att, vv[:, vs], preferred_element_type=F32)
            o = o + lax.dot_general(q_in[:, hs], st.astype(BF16), nt, preferred_element_type=F32)
            st_ref[h] = st * dec[:, hs] + lax.dot_general(vv[:, vs], k_st[:, hs], tn, preferred_element_type=F32)
            if fuse_out:
                tot = o + oo_ref[rs, vs]
                yn = tot * lax.rsqrt(jnp.mean(tot * tot, axis=-1, keepdims=True) + NORM_EPS) * nw
                rf = r_ref[rs, vs].astype(F32)
                y_ref[rs, vs] = (yn * (rf * jax.nn.sigmoid(rf))).astype(y_ref.dtype)
            else:
                y_ref[rs, vs] = o
        return carry

    lax.fori_loop(0, nchunk, chunk, 0)

    @pl.when(i == pl.num_programs(1) - 1)
    def _():
        sout_ref[...] = st_ref[...]


def _gla_call(p, aw_pad, ab, s0, o_other, norm_w, *, n_batch, n_seq, row0, reverse):
    fuse_out = o_other is not None
    tb = min(n_seq, 512)
    nblk = n_seq // tb
    rb0 = row0 // tb

    def blk(i):
        return (nblk - 1 - i) if reverse else i

    def prow(width, col):
        return pl.BlockSpec((tb, width), lambda b, i: (rb0 + b * nblk + blk(i), col // width))

    def orow(width):
        return pl.BlockSpec((tb, width), lambda b, i: (b * nblk + blk(i), 0))

    st_spec = pl.BlockSpec((None, GLA_HEADS, GLA_DV, GLA_DK), lambda b, i: (b, 0, 0, 0))
    nq = GLA_HEADS * GLA_DK
    in_specs = [prow(nq, COL_GLA_Q), prow(nq, COL_GLA_K), prow(MIX_WIDTH, COL_GLA_V), prow(LANES, COL_DECAY),
                pl.BlockSpec((LANES, nq), lambda b, i: (0, 0)), pl.BlockSpec((1, nq), lambda b, i: (0, 0)), st_spec]
    args = [p, p, p, p, aw_pad, ab, s0]
    if fuse_out:
        in_specs += [orow(MIX_WIDTH), prow(MIX_WIDTH, COL_GLA_R), pl.BlockSpec((1, GLA_DV), lambda b, i: (0, 0))]
        args += [o_other, p, norm_w.reshape(1, GLA_DV)]
    return pl.pallas_call(
        functools.partial(_gla_kernel, reverse=reverse, fuse_out=fuse_out, tb=tb),
        grid=(n_batch, nblk),
        in_specs=in_specs,
        out_specs=[orow(MIX_WIDTH), st_spec],
        out_shape=[jax.ShapeDtypeStruct((n_batch * n_seq, MIX_WIDTH), BF16 if fuse_out else F32),
                   jax.ShapeDtypeStruct((n_batch, GLA_HEADS, GLA_DV, GLA_DK), F32)],
        scratch_shapes=[pltpu.VMEM((GLA_HEADS, GLA_DV, GLA_DK), F32), pltpu.VMEM((tb, nq), F32)],
        compiler_params=_cparams(("arbitrary", "arbitrary")),
        name="gla_" + ("bwd" if reverse else "fwd") + ("_out" if fuse_out else ""),
    )(*args)


def _gla_decay_weights(a_w, a_b):
    nq = GLA_HEADS * GLA_DK
    pads = []
    for d in range(2):
        z = jnp.zeros((LANES, nq), F32).at[d * GLA_RANK:(d + 1) * GLA_RANK].set(a_w[d])
        pads.append(z.astype(BF16))
    return pads, [a_b[0].reshape(1, nq), a_b[1].reshape(1, nq)]


def _gla_layer(p, a_w, a_b, norm_w, *, n_batch, n_seq, n_ctx, need_ctx):
    (aw_f, aw_b), (ab_f, ab_b) = _gla_decay_weights(a_w, a_b)
    t_lat = n_batch * n_seq
    zero = jnp.zeros((n_batch, GLA_HEADS, GLA_DV, GLA_DK), F32)
    kw_c = dict(n_batch=n_batch, n_seq=n_ctx, row0=t_lat)
    kw_l = dict(n_batch=n_batch, n_seq=n_seq, row0=0)
    oc_b, s_b = _gla_call(p, aw_b, ab_b, zero, None, None, reverse=True, **kw_c)
    if need_ctx:
        y_cc, s_f = _gla_call(p, aw_f, ab_f, zero, oc_b, norm_w, reverse=False, **kw_c)
    else:
        y_cc = None
        _, s_f = _gla_call(p, aw_f, ab_f, zero, None, None, reverse=False, **kw_c)
    ol_b, _ = _gla_call(p, aw_b, ab_b, s_b, None, None, reverse=True, **kw_l)
    y_c, _ = _gla_call(p, aw_f, ab_f, s_f, ol_b, norm_w, reverse=False, **kw_l)
    return y_c, y_cc


HY_N2 = 128
HY_J = SUBLANES
HY_G = HY_N2 // HY_J
HY_CB = 128
HY_TILE = 2 * HY_J


def _dot3(a, b):
    a_hi, a_lo = _split_bf16(a)
    b_hi, b_lo = _split_bf16(b)
    acc = jnp.dot(a_hi, b_hi, preferred_element_type=F32)
    acc += jnp.dot(a_lo, b_hi, preferred_element_type=F32)
    acc += jnp.dot(a_hi, b_lo, preferred_element_type=F32)
    return acc


def _hy_filter_kernel(z_ref, w1_ref, b1_ref, fr_ref, w2_ref, b2_ref, w3_ref, dec_ref, o_ref):
    z = z_ref[...]
    fr = fr_ref[...]
    h = jnp.sin(fr * (_dot3(z, w1_ref[...]) + b1_ref[...]))
    h = jnp.sin(fr * (_dot3(h, w2_ref[...]) + b2_ref[...]))
    t = z[:, 0:1]
    o_ref[...] = _dot3(h, w3_ref[...]) * jnp.exp(-t * jnp.abs(dec_ref[...]))


def _hy_filter_call(seq_len, w1, b1, freq, w2, b2, w3, decay):
    pos = np.arange(seq_len, dtype=np.float32)
    t = pos / np.float32(max(seq_len - 1, 1))
    ang = (np.float32(2.0 * math.pi) * pos / np.float32(seq_len)).astype(np.float32)
    bands = np.linspace(1e-4, HY_BANDS - 1, HY_BANDS, dtype=np.float32)
    fw = (ang[:, None] * bands[None, :]).astype(np.float32)
    z = jnp.concatenate([jnp.asarray(t)[:, None], jnp.cos(jnp.asarray(fw)), -jnp.sin(jnp.asarray(fw))], axis=-1)
    z = jnp.pad(z, ((0, 0), (0, LANES - HY_EMB)))
    padh = LANES - HY_FFN
    w1p = jnp.pad(w1, ((0, LANES - HY_EMB), (0, padh)))
    w2p = jnp.pad(w2, ((0, padh), (0, padh)))
    w3p = jnp.pad(w3, ((0, padh), (0, 0)))
    row = lambda v: jnp.pad(v, (0, padh)).reshape(1, LANES)
    n_out = w3.shape[1]
    tl = min(seq_len, 256)
    full = lambda shape: pl.BlockSpec(shape, lambda i: (0, 0))
    return pl.pallas_call(
        _hy_filter_kernel,
        grid=(seq_len // tl,),
        in_specs=[pl.BlockSpec((tl, LANES), lambda i: (i, 0)), full((LANES, LANES)), full((1, LANES)),
                  full((1, LANES)), full((LANES, LANES)), full((1, LANES)), full((LANES, n_out)), full((1, n_out))],
        out_specs=pl.BlockSpec((tl, n_out), lambda i: (i, 0)),
        out_shape=jax.ShapeDtypeStruct((seq_len, n_out), F32),
        compiler_params=_cparams(("arbitrary",)),
        name="hy_filter",
    )(z, w1p, row(b1), row(freq), w2p, row(b2), w3p, decay.reshape(1, n_out))


@functools.lru_cache(maxsize=None)
def _hy_constants(seq_len):
    nfft = 2 * seq_len
    n2, jj, g = HY_N2, HY_J, HY_G
    n1 = nfft // n2
    h1 = n1 // 2
    k1 = np.arange(n1)

    def kron_mat(rows_n, cols_n, sign, both_parts_in):
        m = np.zeros((rows_n, 2, jj, cols_n, 2 if both_parts_in else 1, jj))
        for j in range(jj):
            r = np.arange(rows_n)[:, None]
            c = np.arange(cols_n)[None, :]
            kk = r if sign < 0 else c
            ph = sign * 2.0 * np.pi * (r * c / n1 + kk * j / nfft)
            cr, ci = np.cos(ph), np.sin(ph)
            m[:, 0, j, :, 0, j] = cr
            m[:, 1, j, :, 0, j] = ci
            if both_parts_in:
                m[:, 0, j, :, 1, j] = -ci
                m[:, 1, j, :, 1, j] = cr
        return m.reshape(rows_n * 2 * jj, -1)

    ma = kron_mat(n1, h1, -1.0, True)
    mak = kron_mat(n1, h1, -1.0, False)
    mai = kron_mat(h1, n1, +1.0, True)

    nn = (np.arange(g)[:, None] * jj + np.arange(jj)[None, :])
    kk2 = np.arange(n2)
    ph = -2.0 * np.pi * kk2[:, None, None] * nn[None] / n2
    mc = np.zeros((2, n2, g, 2, jj))
    mc[0, :, :, 0, :] = np.cos(ph)
    mc[0, :, :, 1, :] = -np.sin(ph)
    mc[1, :, :, 0, :] = np.sin(ph)
    mc[1, :, :, 1, :] = np.cos(ph)
    mc = mc.reshape(2 * n2, 2 * n2)
    mci = np.zeros((g, 2, jj, 2, n2))
    phi = np.transpose(-ph, (1, 2, 0))
    mci[:, 0, :, 0, :] = np.cos(phi)
    mci[:, 0, :, 1, :] = -np.sin(phi)
    mci[:, 1, :, 0, :] = np.sin(phi)
    mci[:, 1, :, 1, :] = np.cos(phi)
    mci = mci.reshape(2 * n2, 2 * n2)

    th = 2.0 * np.pi * (np.arange(g)[:, None] * jj) * k1[None, :] / nfft
    twc = np.cos(th).astype(np.float32).reshape(-1)
    tws = (-np.sin(th)).astype(np.float32).reshape(-1)
    to_bf = lambda m: m.astype(np.float32).astype(BF16)
    return dict(ma=to_bf(ma), mak=to_bf(mak), mai=to_bf(mai), mc=to_bf(mc), mci=to_bf(mci), twc=twc, tws=tws,
                n1=n1, h1=h1)


def _hy_twiddle_tiles(y, g, twc_ref, tws_ref, n1, conj):
    tiles = []
    for k1 in range(n1):
        c = twc_ref[g * n1 + k1]
        s = tws_ref[g * n1 + k1]
        if conj:
            s = -s
        yr = y[HY_TILE * k1:HY_TILE * k1 + HY_J]
        yi = y[HY_TILE * k1 + HY_J:HY_TILE * (k1 + 1)]
        tiles.append(jnp.concatenate([c * yr - s * yi, s * yr + c * yi], axis=0).astype(BF16))
    return tiles


def _hy_tile_rows(k1, g):
    return pl.ds(pl.multiple_of(k1 * 2 * HY_N2 + g * HY_TILE, HY_TILE), HY_TILE)


def _hy_khat_kernel(twc_ref, tws_ref, hf_ref, hb_ref, mak_ref, mc_ref, kh_ref, a_ref, *, n1):
    nfft = n1 * HY_N2
    h1 = n1 // 2
    cb = hf_ref.shape[1]

    def stage_a(g, carry):
        def slab(ref, r):
            return ref[pl.ds(pl.multiple_of(r * HY_N2 + g * HY_J, HY_J), HY_J), :]

        fs = [slab(hf_ref, r) for r in range(h1)]
        bs = [slab(hb_ref, r) for r in range(h1)]
        first = (lax.broadcasted_iota(jnp.int32, (HY_J, cb), 0) == 0) & (g == 0)
        bs[0] = jnp.where(first, 0.0, bs[0])
        rhs = jnp.concatenate([jnp.concatenate(fs, axis=0), jnp.concatenate(bs, axis=0)], axis=1).astype(BF16)
        y = jnp.dot(mak_ref[...], rhs, preferred_element_type=F32)
        for k1, t in enumerate(_hy_twiddle_tiles(y, g, twc_ref, tws_ref, n1, False)):
            a_ref[_hy_tile_rows(k1, g), :] = t
        return carry

    lax.fori_loop(0, HY_G, stage_a, 0)

    def stage_c(k1, carry):
        blk = pl.ds(pl.multiple_of(k1 * 2 * HY_N2, 2 * HY_N2), 2 * HY_N2)
        x = jnp.dot(mc_ref[...], a_ref[blk, :], preferred_element_type=F32) * (1.0 / nfft)
        xf, xb = x[:, :cb], x[:, cb:]
        kh_ref[blk, :] = jnp.concatenate([xf[:HY_N2] + xb[:HY_N2], xf[HY_N2:] - xb[HY_N2:]], axis=0).astype(BF16)
        return carry

    lax.fori_loop(0, n1, stage_c, 0)


def _hy_khat_call(h, consts):
    seq_len, c2 = h.shape
    c = c2 // 2
    n1 = consts["n1"]
    rows = n1 * 2 * HY_N2
    ncb = c // HY_CB
    return pl.pallas_call(
        functools.partial(_hy_khat_kernel, n1=n1),
        grid=(ncb,),
        in_specs=[pl.BlockSpec(memory_space=pltpu.SMEM), pl.BlockSpec(memory_space=pltpu.SMEM),
                  pl.BlockSpec((seq_len, HY_CB), lambda i: (0, i)),
                  pl.BlockSpec((seq_len, HY_CB), lambda i: (0, ncb + i)),
                  pl.BlockSpec(consts["mak"].shape, lambda i: (0, 0)),
                  pl.BlockSpec(consts["mc"].shape, lambda i: (0, 0))],
        out_specs=pl.BlockSpec((rows, HY_CB), lambda i: (0, i)),
        out_shape=jax.ShapeDtypeStruct((rows, c), BF16),
        scratch_shapes=[pltpu.VMEM((rows, 2 * HY_CB), BF16)],
        compiler_params=_cparams(("arbitrary",)),
        name="hy_khat",
    )(jnp.asarray(consts["twc"]), jnp.asarray(consts["tws"]), h, h, jnp.asarray(consts["mak"]),
      jnp.asarray(consts["mc"]))


def _hy_conv3_chunk(ref, base, w, b, r, n_chunks, seq_len):
    r0 = pl.multiple_of(r * HY_N2, HY_N2)
    cur = ref[pl.ds(base + r0, HY_N2), :].astype(F32)
    pr0 = pl.multiple_of(jnp.maximum(r0 - HY_TILE, 0), HY_TILE)
    nx0 = pl.multiple_of(jnp.minimum(r0 + HY_N2, seq_len - HY_TILE), HY_TILE)
    prev_row = ref[pl.ds(base + pr0, HY_TILE), :].astype(F32)[HY_TILE - 1:HY_TILE]
    next_row = ref[pl.ds(base + nx0, HY_TILE), :].astype(F32)[0:1]
    prev_row = jnp.where(r > 0, prev_row, 0.0)
    next_row = jnp.where(r < n_chunks - 1, next_row, 0.0)
    row = lax.broadcasted_iota(jnp.int32, cur.shape, 0)
    up = jnp.where(row == 0, prev_row, pltpu.roll(cur, 1, 0))
    dn = jnp.where(row == HY_N2 - 1, next_row, pltpu.roll(cur, HY_N2 - 1, 0))
    return up * w[0:1] + cur * w[1:2] + dn * w[2:3] + b


def _hy_interleave(a, b):
    parts = []
    for g in range(HY_G):
        parts.append(a[g * HY_J:(g + 1) * HY_J])
        parts.append(b[g * HY_J:(g + 1) * HY_J])
    return jnp.concatenate(parts, axis=0)


def _hy_conv_kernel(twc_ref, tws_ref, x0_ref, x1_ref, v_ref,
                    cw0_ref, cw1_ref, cw2_ref, cb0_ref, cb1_ref, cb2_ref, bias_ref, kh_ref,
                    ma_ref, mc_ref, mci_ref, mai_ref, o_ref, zc_ref, x0s_ref, a_ref, *, n1, seq_len):
    h1 = n1 // 2
    tile_rows = 2 * HY_N2

    def prologue(r, carry):
        cw0, cw1, cw2 = cw0_ref[...], cw1_ref[...], cw2_ref[...]
        cb0, cb1, cb2 = cb0_ref[...], cb1_ref[...], cb2_ref[...]
        zs, x0s = [], []
        for base in (0, seq_len):
            x0s.append(_hy_conv3_chunk(x0_ref, base, cw0, cb0, r, h1, seq_len))
            x1c = _hy_conv3_chunk(x1_ref, base, cw1, cb1, r, h1, seq_len)
            vc = _hy_conv3_chunk(v_ref, base, cw2, cb2, r, h1, seq_len)
            zs.append(x1c * vc)
        dst = pl.ds(pl.multiple_of(r * tile_rows, tile_rows), tile_rows)
        zc_ref[dst, :] = _hy_interleave(zs[0], zs[1]).astype(BF16)
        x0s_ref[dst, :] = _hy_interleave(x0s[0], x0s[1]).astype(BF16)
        return carry

    lax.fori_loop(0, h1, prologue, 0)

    cb = o_ref.shape[1]
    ga = 2
    kc = min(4, n1)

    def stage_a(gp, carry):
        groups = [gp * ga + u for u in range(ga)]
        rhs = jnp.concatenate(
            [jnp.concatenate([zc_ref[_hy_tile_rows(r, g), :] for r in range(h1)], axis=0) for g in groups], axis=1)
        y = jnp.dot(ma_ref[...], rhs, preferred_element_type=F32)
        for u, g in enumerate(groups):
            for k1, t in enumerate(_hy_twiddle_tiles(y[:, u * cb:(u + 1) * cb], g, twc_ref, tws_ref, n1, False)):
                a_ref[_hy_tile_rows(k1, g), :] = t
        return carry

    lax.fori_loop(0, HY_G // ga, stage_a, 0)

    def stage_c(kq, carry):
        blks = [pl.ds(pl.multiple_of((kq * kc + u) * tile_rows, tile_rows), tile_rows) for u in range(kc)]
        x = jnp.dot(mc_ref[...], jnp.concatenate([a_ref[b, :] for b in blks], axis=1), preferred_element_type=F32)
        pms = []
        for u, b in enumerate(blks):
            kh = kh_ref[b, :].astype(F32)
            xr, xi = x[:HY_N2, u * cb:(u + 1) * cb], x[HY_N2:, u * cb:(u + 1) * cb]
            kr, ki = kh[:HY_N2], kh[HY_N2:]
            pms.append(jnp.concatenate([xr * kr - xi * ki, xr * ki + xi * kr], axis=0).astype(BF16))
        z = jnp.dot(mci_ref[...], jnp.concatenate(pms, axis=1), preferred_element_type=F32).astype(BF16)
        for u, b in enumerate(blks):
            a_ref[b, :] = z[:, u * cb:(u + 1) * cb]
        return carry

    lax.fori_loop(0, n1 // kc, stage_c, 0)

    bias = bias_ref[...]

    def stage_ai(gp, carry):
        groups = [gp * ga + u for u in range(ga)]
        cols = []
        for g in groups:
            tiles = []
            for k1 in range(n1):
                t = a_ref[_hy_tile_rows(k1, g), :].astype(F32)
                c = twc_ref[g * n1 + k1]
                s = -tws_ref[g * n1 + k1]
                tr, ti = t[:HY_J], t[HY_J:]
                tiles.append(jnp.concatenate([c * tr - s * ti, s * tr + c * ti], axis=0).astype(BF16))
            cols.append(jnp.concatenate(tiles, axis=0))
        y = jnp.dot(mai_ref[...], jnp.concatenate(cols, axis=1), preferred_element_type=F32)
        for u, g in enumerate(groups):
            for r in range(h1):
                src = _hy_tile_rows(r, g)
                z = zc_ref[src, :].astype(F32)
                x0 = x0s_ref[src, :].astype(F32)
                o = x0 * (y[HY_TILE * r:HY_TILE * (r + 1), u * cb:(u + 1) * cb] + bias * z)
                dst = pl.ds(pl.multiple_of(r * HY_N2 + g * HY_J, HY_J), HY_J)
                o_ref[dst, :] = o[:HY_J]
                dst2 = pl.ds(pl.multiple_of(seq_len + r * HY_N2 + g * HY_J, HY_J), HY_J)
                o_ref[dst2, :] = o[HY_J:]
        return carry

    lax.fori_loop(0, HY_G // ga, stage_ai, 0)


def _hy_conv_call(p, khat, conv_w, conv_b, bias, consts, *, n_batch, seq_len, row0):
    c = HY_WIDTH
    n1 = consts["n1"]
    h1 = consts["h1"]
    rb0 = row0 // (2 * seq_len)
    ncb = c // HY_CB
    tile_rows = 2 * HY_N2

    def pin(col):
        return pl.BlockSpec((2 * seq_len, HY_CB), lambda cb, pi: (rb0 + pi, col // HY_CB + cb))

    def cws(stream):
        return pl.BlockSpec((3, HY_CB), lambda cb, pi: (0, stream * ncb + cb))

    def cbs(stream):
        return pl.BlockSpec((1, HY_CB), lambda cb, pi: (0, stream * ncb + cb))

    const = lambda a: pl.BlockSpec(a.shape, lambda cb, pi: (0, 0))
    mats = [jnp.asarray(consts[k]) for k in ("ma", "mc", "mci", "mai")]
    in_specs = [pl.BlockSpec(memory_space=pltpu.SMEM), pl.BlockSpec(memory_space=pltpu.SMEM),
                pin(COL_HY_X0), pin(COL_HY_X1), pin(COL_HY_V), cws(0), cws(1), cws(2), cbs(0), cbs(1), cbs(2),
                pl.BlockSpec((1, HY_CB), lambda cb, pi: (0, cb)),
                pl.BlockSpec((n1 * tile_rows, HY_CB), lambda cb, pi: (0, cb))] + [const(m) for m in mats]
    return pl.pallas_call(
        functools.partial(_hy_conv_kernel, n1=n1, seq_len=seq_len),
        grid=(ncb, n_batch // 2),
        in_specs=in_specs,
        out_specs=pl.BlockSpec((2 * seq_len, HY_CB), lambda cb, pi: (pi, cb)),
        out_shape=jax.ShapeDtypeStruct((n_batch * seq_len, c), F32),
        scratch_shapes=[pltpu.VMEM((h1 * tile_rows, HY_CB), BF16), pltpu.VMEM((h1 * tile_rows, HY_CB), BF16),
                        pltpu.VMEM((n1 * tile_rows, HY_CB), BF16)],
        compiler_params=_cparams(("arbitrary", "arbitrary"), 56),
        name="hy_conv",
    )(jnp.asarray(consts["twc"]), jnp.asarray(consts["tws"]), p, p, p,
      conv_w, conv_w, conv_w, conv_b.reshape(1, -1), conv_b.reshape(1, -1), conv_b.reshape(1, -1),
      bias.reshape(1, -1), khat, *mats)


def _hyena_layer(p, params, *, n_batch, seq_len, row0):
    conv_w, conv_b, w1, b1, freq, w2, b2, w3, decay, bias = params
    consts = _hy_constants(seq_len)
    h = _hy_filter_call(seq_len, w1, b1, freq, w2, b2, w3, decay)
    khat = _hy_khat_call(h, consts)
    return _hy_conv_call(p, khat, conv_w, conv_b, bias, consts, n_batch=n_batch, seq_len=seq_len, row0=row0)


ROW_WORDS = D_MODEL // 2
ROW_SUB = ROW_WORDS // LANES


def _pack_rows(val, ref2d, n_rows):
    lo = lax.bitcast_convert_type(val[:, :ROW_WORDS].astype(BF16).astype(F32), jnp.uint32)
    hi = lax.bitcast_convert_type(val[:, ROW_WORDS:].astype(BF16).astype(F32), jnp.uint32)
    word = (hi & jnp.uint32(0xFFFF0000)) | (lo >> 16)
    for s in range(ROW_SUB):
        ref2d[pl.ds(s, n_rows, stride=ROW_SUB), :] = word[:, s * LANES:(s + 1) * LANES]


def _unpack_words(w):
    lo = lax.bitcast_convert_type(w << 16, F32)
    hi = lax.bitcast_convert_type(w & jnp.uint32(0xFFFF0000), F32)
    return lo, hi


def _merge_kernel(*refs, has_ctx, n_lat_tiles, tm):
    if has_ctx:
        (x_ref, g1_ref, sh2_ref, sc2_ref, nw2_ref, gl0_ref, gl1_ref, gl2_ref, ya_l, yb_l, yc_l, ya_c, yb_c, yc_c,
         wb_ref, wo_ref, rw_ref, rb_ref, xo_ref, h2p_ref, lg_ref) = refs
    else:
        (x_ref, g1_ref, sh2_ref, sc2_ref, nw2_ref, gl0_ref, gl1_ref, gl2_ref, ya_l, yb_l, yc_l,
         wb_ref, wo_ref, rw_ref, rb_ref, xo_ref, h2p_ref, lg_ref) = refs
    ys = [ya_l[...].astype(BF16), yb_l[...], yc_l[...]]
    if has_ctx:
        is_ctx = pl.program_id(0) >= n_lat_tiles
        ys = [jnp.where(is_ctx, c, l) for c, l in zip((ya_c[...].astype(BF16), yb_c[...], yc_c[...]), ys)]
    m = None
    for g, gl_ref in enumerate((gl0_ref, gl1_ref, gl2_ref)):
        proj = jnp.dot(ys[g], wb_ref[g], preferred_element_type=F32)
        term = jax.nn.sigmoid(gl_ref[...].astype(F32)) * proj
        m = term if m is None else m + term
    out = jnp.dot(m.astype(BF16), wo_ref[...], preferred_element_type=F32)
    xn = x_ref[...] + g1_ref[...] * out
    xo_ref[...] = xn
    h2 = _rms_modulate(xn, nw2_ref[...], sh2_ref[...], 1.0 + sc2_ref[...])
    lg_ref[...] = _dot3(h2, rw_ref[...]) + rb_ref[...]
    _pack_rows(h2, h2p_ref, tm)


def _merge_call(xf, p, mods_l, norm2_w, ys_lat, ys_ctx, wb_bf, wo_bf, rw_pad, rb_pad, *, n_batch, n_seq, n_rows):
    d = D_MODEL
    has_ctx = ys_ctx is not None
    tm = 256
    tpb = n_seq // tm
    nlt = n_batch * tpb
    nt = n_rows // tm

    def mod_spec(comp):
        return pl.BlockSpec((None, None, 1, d), lambda i: (_mod_row(i, tpb, nlt, n_batch), comp, 0, 0))

    def gate_spec(g):
        return pl.BlockSpec((tm, d), lambda i: (i, COL_GATES // d + g))

    lat_spec = pl.BlockSpec((tm, MIX_WIDTH), lambda i: (jnp.minimum(i, nlt - 1), 0))
    ctx_spec = pl.BlockSpec((tm, MIX_WIDTH), lambda i: (jnp.maximum(i - nlt, 0), 0))
    once = pl.Buffered(1)
    in_specs = [pl.BlockSpec((tm, d), lambda i: (i, 0)), mod_spec(2), mod_spec(3), mod_spec(4),
                pl.BlockSpec((1, d), lambda i: (0, 0)), gate_spec(0), gate_spec(1), gate_spec(2),
                lat_spec, lat_spec, lat_spec]
    args = [xf, mods_l, mods_l, mods_l, norm2_w.reshape(1, d), p, p, p, *ys_lat]
    if has_ctx:
        in_specs += [ctx_spec, ctx_spec, ctx_spec]
        args += list(ys_ctx)
    in_specs += [pl.BlockSpec((N_BRANCH, MIX_WIDTH, d), lambda i: (0, 0, 0), pipeline_mode=once),
                 pl.BlockSpec((d, d), lambda i: (0, 0), pipeline_mode=once),
                 pl.BlockSpec((d, LANES), lambda i: (0, 0)), pl.BlockSpec((1, LANES), lambda i: (0, 0))]
    args += [wb_bf, wo_bf, rw_pad, rb_pad]
    return pl.pallas_call(
        functools.partial(_merge_kernel, has_ctx=has_ctx, n_lat_tiles=nlt, tm=tm),
        grid=(nt,),
        in_specs=in_specs,
        out_specs=[pl.BlockSpec((tm, d), lambda i: (i, 0)),
                   pl.BlockSpec((tm * ROW_SUB, LANES), lambda i: (i, 0)),
                   pl.BlockSpec((tm, LANES), lambda i: (i, 0))],
        out_shape=[jax.ShapeDtypeStruct((n_rows, d), F32),
                   jax.ShapeDtypeStruct((n_rows * ROW_SUB, LANES), jnp.uint32),
                   jax.ShapeDtypeStruct((n_rows, LANES), F32)],
        compiler_params=_cparams(("arbitrary",), 58),
        name="merge",
    )(*args)


def _route_kernel(lg_ref, idx_ref, rank_ref, gate_ref, cnt_ref, carry_ref, *, tr):
    i = pl.program_id(0)

    @pl.when(i == 0)
    def _():
        carry_ref[...] = jnp.zeros_like(carry_ref)

    lane = lax.broadcasted_iota(jnp.int32, (tr, LANES), 1)
    lg = jnp.where(lane < N_EXPERTS, lg_ref[...], -jnp.inf)
    tops, hots = [], []
    for _ in range(TOP_K):
        mx = jnp.max(lg, axis=-1, keepdims=True)
        sel = jnp.min(jnp.where(lg == mx, lane, LANES), axis=-1, keepdims=True)
        hot = lane == sel
        tops.append((mx, sel))
        hots.append(hot)
        lg = jnp.where(hot, -jnp.inf, lg)
    es = [jnp.exp(mx - tops[0][0]) for mx, _ in tops]
    den = es[0] + es[1] + es[2] + es[3]
    hot_f = [jnp.where(h, 1.0, 0.0) for h in hots]
    chosen = hot_f[0] + hot_f[1] + hot_f[2] + hot_f[3]
    r_i = lax.broadcasted_iota(jnp.int32, (tr, tr), 0)
    c_i = lax.broadcasted_iota(jnp.int32, (tr, tr), 1)
    below = jnp.where(c_i < r_i, 1.0, 0.0).astype(BF16)
    before = jnp.dot(below, chosen.astype(BF16), preferred_element_type=F32) + carry_ref[...]
    idx_o = jnp.zeros((tr, LANES), jnp.int32)
    rank_o = jnp.zeros((tr, LANES), jnp.int32)
    gate_o = jnp.zeros((tr, LANES), F32)
    for j in range(TOP_K):
        rank_j = jnp.sum(hot_f[j] * before, axis=-1, keepdims=True).astype(jnp.int32)
        idx_o = jnp.where(lane == j, tops[j][1], idx_o)
        rank_o = jnp.where(lane == j, rank_j, rank_o)
        gate_o = jnp.where(lane == j, es[j] / den, gate_o)
    idx_ref[...] = idx_o
    rank_ref[...] = rank_o
    gate_ref[...] = gate_o
    carry_ref[...] += jnp.sum(chosen, axis=0, keepdims=True)
    cnt_ref[...] = carry_ref[...]


def _route_call(logits):
    t_rows = logits.shape[0]
    tr = 512
    tile = pl.BlockSpec((tr, LANES), lambda i: (i, 0))
    return pl.pallas_call(
        functools.partial(_route_kernel, tr=tr),
        grid=(t_rows // tr,),
        in_specs=[tile],
        out_specs=[tile, tile, tile, pl.BlockSpec((1, LANES), lambda i: (0, 0))],
        out_shape=[jax.ShapeDtypeStruct((t_rows, LANES), jnp.int32), jax.ShapeDtypeStruct((t_rows, LANES), jnp.int32),
                   jax.ShapeDtypeStruct((t_rows, LANES), F32), jax.ShapeDtypeStruct((1, LANES), F32)],
        scratch_shapes=[pltpu.VMEM((1, LANES), F32)],
        compiler_params=_cparams(("arbitrary",)),
        name="route",
    )(logits)


MOE_BLK = 256


def _row_tile(ref2d, row):
    return ref2d.at[pl.ds(pl.multiple_of(row * ROW_SUB, ROW_SUB), ROW_SUB), :]


def _dispatch_kernel(dest_ref, cnt_ref, pad_ref, off_ref, h_ref, xs_ref, zero_ref, sem, zsem, *, td):
    i = pl.program_id(0)

    @pl.when(i == 0)
    def _():
        zero_ref[...] = jnp.zeros_like(zero_ref)
        for e in range(N_EXPERTS):
            base = off_ref[e]

            def fill(r, carry):
                pltpu.make_async_copy(zero_ref, _row_tile(xs_ref, base + r), zsem).start()
                return carry

            def drain(r, carry):
                pltpu.make_async_copy(zero_ref, _row_tile(xs_ref, base + r), zsem).wait()
                return carry

            lax.fori_loop(cnt_ref[e], pad_ref[e], fill, 0)
            lax.fori_loop(cnt_ref[e], pad_ref[e], drain, 0)

    def issue(r, carry):
        src = _row_tile(h_ref, r)
        for j in range(TOP_K):
            pltpu.make_async_copy(src, _row_tile(xs_ref, dest_ref[r * TOP_K + j]), sem).start()
        return carry

    lax.fori_loop(0, td, issue, 0)

    def drain_all(r, carry):
        src = _row_tile(h_ref, r)
        for j in range(TOP_K):
            pltpu.make_async_copy(src, _row_tile(xs_ref, dest_ref[r * TOP_K + j]), sem).wait()
        return carry

    lax.fori_loop(0, td, drain_all, 0)


def _dispatch_call(h2p, dest_flat, counts, padded, offsets, n_rows_sorted):
    t_rows = h2p.shape[0] // ROW_SUB
    td = 256
    smem = pl.BlockSpec(memory_space=pltpu.SMEM)
    return pl.pallas_call(
        functools.partial(_dispatch_kernel, td=td),
        grid=(t_rows // td,),
        in_specs=[pl.BlockSpec((td * TOP_K,), lambda i: (i,), memory_space=pltpu.SMEM), smem, smem, smem,
                  pl.BlockSpec((td * ROW_SUB, LANES), lambda i: (i, 0))],
        out_specs=pl.BlockSpec(memory_space=pl.ANY),
        out_shape=jax.ShapeDtypeStruct((n_rows_sorted * ROW_SUB, LANES), jnp.uint32),
        scratch_shapes=[pltpu.VMEM((ROW_SUB, LANES), jnp.uint32), pltpu.SemaphoreType.DMA(()),
                        pltpu.SemaphoreType.DMA(())],
        compiler_params=_cparams(("arbitrary",)),
        name="moe_dispatch",
    )(dest_flat, counts, padded, offsets, h2p)


def _expert_kernel(be_ref, na_ref, x_ref, w1_ref, b1_ref, w2_ref, b2_ref, y_ref, w1s_ref, w2s_ref):
    b = pl.program_id(0)
    prev = be_ref[jnp.maximum(b - 1, 0)]
    fresh = (b == 0) | (be_ref[b] != prev)

    @pl.when(fresh & (b < na_ref[0]))
    def _():
        for s in range(ROW_SUB):
            w1s_ref[s, :LANES, :] = w1_ref[s * LANES:(s + 1) * LANES, :].astype(BF16)
            w1s_ref[s, LANES:, :] = w1_ref[ROW_WORDS + s * LANES:ROW_WORDS + (s + 1) * LANES, :].astype(BF16)
        w2s_ref[...] = w2_ref[...].astype(BF16)

    @pl.when(b < na_ref[0])
    def _():
        acc = None
        for s in range(ROW_SUB):
            lo, hi = _unpack_words(x_ref[pl.ds(s, MOE_BLK, stride=ROW_SUB), :])
            xc = jnp.concatenate([lo.astype(BF16), hi.astype(BF16)], axis=1)
            part = jnp.dot(xc, w1s_ref[s], preferred_element_type=F32)
            acc = part if acc is None else acc + part
        a = acc + b1_ref[...]
        a_glu = jnp.minimum(a[:, :EXPERT_FF], SWIGLU_LIMIT)
        a_lin = jnp.clip(a[:, EXPERT_FF:], -SWIGLU_LIMIT, SWIGLU_LIMIT)
        act = a_glu * jax.nn.sigmoid(SWIGLU_ALPHA * a_glu) * (a_lin + 1.0)
        y = jnp.dot(act.astype(BF16), w2s_ref[...], preferred_element_type=F32) + b2_ref[...]
        _pack_rows(y, y_ref, MOE_BLK)


def _expert_call(xs, block_e, n_active, w1, b1, w2, b2, layer):
    n_blocks = block_e.shape[0]
    depth, n_exp, d, ff2 = w1.shape
    ff = w2.shape[2]

    def rows(b, be, na):
        return (jnp.minimum(b, na[0] - 1), 0)

    def wspec(r, c):
        return pl.BlockSpec((None, None, r, c), lambda b, be, na: (layer, be[b], 0, 0))

    return pl.pallas_call(
        _expert_kernel,
        grid_spec=pltpu.PrefetchScalarGridSpec(
            num_scalar_prefetch=2,
            grid=(n_blocks,),
            in_specs=[pl.BlockSpec((MOE_BLK * ROW_SUB, LANES), rows),
                      wspec(d, ff2), wspec(1, ff2), wspec(ff, d), wspec(1, d)],
            out_specs=pl.BlockSpec((MOE_BLK * ROW_SUB, LANES), rows),
            scratch_shapes=[pltpu.VMEM((ROW_SUB, 2 * LANES, ff2), BF16), pltpu.VMEM((ff, d), BF16)],
        ),
        out_shape=jax.ShapeDtypeStruct(xs.shape, jnp.uint32),
        compiler_params=_cparams(("arbitrary",), 56),
        name="moe_experts",
    )(block_e, n_active, xs, w1, b1.reshape(depth, n_exp, 1, ff2), w2, b2.reshape(depth, n_exp, 1, d))


def _combine_kernel(dest_ref, dest_next_ref, gate_ref, x_ref, g2_ref, fw_ref, ys_ref, o_ref, buf0_ref, buf1_ref,
                    sems, *, tc, final_norm):
    i = pl.program_id(0)
    n = pl.num_programs(0)
    bufs = (buf0_ref, buf1_ref)

    def row_copies(d_ref, b, r):
        return [pltpu.make_async_copy(_row_tile(ys_ref, d_ref[r * TOP_K + j]), _row_tile(bufs[b], j * tc + r),
                                      sems.at[b]) for j in range(TOP_K)]

    def start_tile(d_ref, b):
        def body(r, carry):
            for cp in row_copies(d_ref, b, r):
                cp.start()
            return carry

        lax.fori_loop(0, tc, body, 0)

    def wait_tile(b):
        def body(r, carry):
            for cp in row_copies(dest_ref, b, r):
                cp.wait()
            return carry

        lax.fori_loop(0, tc, body, 0)

    def reduce_tile(buf_ref):
        gate = gate_ref[...]
        gs = [gate[:, j:j + 1] for j in range(TOP_K)]
        lo_parts, hi_parts = [], []
        for s in range(ROW_SUB):
            f_lo = None
            f_hi = None
            for j in range(TOP_K):
                lo, hi = _unpack_words(buf_ref[pl.ds(j * tc * ROW_SUB + s, tc, stride=ROW_SUB), :])
                f_lo = gs[j] * lo if f_lo is None else f_lo + gs[j] * lo
                f_hi = gs[j] * hi if f_hi is None else f_hi + gs[j] * hi
            lo_parts.append(f_lo)
            hi_parts.append(f_hi)
        f = jnp.concatenate(lo_parts + hi_parts, axis=1)
        xn = x_ref[...] + g2_ref[...] * f
        if final_norm:
            ms = jnp.mean(xn * xn, axis=-1, keepdims=True)
            xn = xn * lax.rsqrt(ms + NORM_EPS) * fw_ref[...]
        o_ref[...] = xn

    @pl.when(i == 0)
    def _():
        start_tile(dest_ref, 0)

    for b in range(2):
        @pl.when((i % 2 == b) & (i + 1 < n))
        def _():
            start_tile(dest_next_ref, 1 - b)

    for b in range(2):
        @pl.when(i % 2 == b)
        def _():
            wait_tile(b)
            reduce_tile(bufs[b])


def _combine_call(xf, ys, dest_flat, gates, mods_l, final_w, *, n_batch, n_seq, final_norm):
    t_rows, d = xf.shape
    tc = 256
    tpb = n_seq // tc
    nlt = n_batch * tpb
    nt = t_rows // tc
    buf = pltpu.VMEM((TOP_K * tc * ROW_SUB, LANES), jnp.uint32)
    return pl.pallas_call(
        functools.partial(_combine_kernel, tc=tc, final_norm=final_norm),
        grid=(nt,),
        in_specs=[pl.BlockSpec((tc * TOP_K,), lambda i: (i,), memory_space=pltpu.SMEM),
                  pl.BlockSpec((tc * TOP_K,), lambda i: (jnp.minimum(i + 1, nt - 1),), memory_space=pltpu.SMEM),
                  pl.BlockSpec((tc, LANES), lambda i: (i, 0)),
                  pl.BlockSpec((tc, d), lambda i: (i, 0)),
                  pl.BlockSpec((None, None, 1, d), lambda i: (_mod_row(i, tpb, nlt, n_batch), 5, 0, 0)),
                  pl.BlockSpec((1, d), lambda i: (0, 0)),
                  pl.BlockSpec(memory_space=pl.ANY)],
        out_specs=pl.BlockSpec((tc, d), lambda i: (i, 0)),
        out_shape=jax.ShapeDtypeStruct((t_rows, d), F32),
        scratch_shapes=[buf, buf, pltpu.SemaphoreType.DMA((2,))],
        compiler_params=_cparams(("arbitrary",)),
        name="moe_combine",
    )(dest_flat, dest_flat, gates, xf, mods_l, final_w.reshape(1, d), ys)


def _moe_layer(xf, h2p, logits, mods_l, w1, b1, w2, b2, layer, final_w, *, n_batch, n_seq, final_norm):
    t_rows = xf.shape[0]
    idx, rank, gates, cnt = _route_call(logits)
    counts = cnt[0, :N_EXPERTS].astype(jnp.int32)
    padded = (counts + MOE_BLK - 1) // MOE_BLK * MOE_BLK
    ends = jnp.cumsum(padded)
    offsets = ends - padded
    dest = (offsets[idx[:, :TOP_K]] + rank[:, :TOP_K]).reshape(-1)
    n_blocks = (t_rows * TOP_K + N_EXPERTS * (MOE_BLK - 1)) // MOE_BLK
    block_row0 = jnp.arange(n_blocks, dtype=jnp.int32) * MOE_BLK
    block_e = jnp.minimum(jnp.sum((ends[None, :] <= block_row0[:, None]).astype(jnp.int32), axis=1), N_EXPERTS - 1)
    n_active = (ends[-1:] // MOE_BLK).astype(jnp.int32)
    xs = _dispatch_call(h2p, dest, counts, padded, offsets, n_blocks * MOE_BLK)
    ys = _expert_call(xs, block_e, n_active, w1, b1, w2, b2, layer)
    return _combine_call(xf, ys, dest, gates, mods_l, final_w, n_batch=n_batch, n_seq=n_seq, final_norm=final_norm)


def _permute_w_in(w_in, layer):
    pad = jnp.zeros((w_in.shape[1], P_PAD - w_in.shape[2]), w_in.dtype)
    pieces = [w_in[layer, :, lo:hi] for lo, hi in
              ((0, 4096), (5376, 7424), (7456, 13600), (4352, 5376), (4096, 4352), (7424, 7456))]
    return jnp.concatenate(pieces + [pad], axis=-1).astype(BF16)


def kernel(x, c, ctx, c_ctx, ada_w, ada_b, norm1_w, norm2_w, w_in, hy_conv_w, hy_conv_b, hy_filt_w1, hy_filt_b1,
           hy_filt_freq, hy_filt_w2, hy_filt_b2, hy_filt_w3, hy_decay, hy_bias, attn_sink, gla_a_w, gla_a_b,
           gla_norm_w, w_branch, w_out, router_w, router_b, moe_w1, moe_b1, moe_w2, moe_b2, final_norm_w):
    n_batch, n_seq, d = x.shape
    n_ctx = ctx.shape[1]
    depth = ada_w.shape[0]
    t_lat = n_batch * n_seq
    xf = jnp.concatenate([x.reshape(t_lat, d), ctx.reshape(n_batch * n_ctx, d)], axis=0)
    cond = jnp.zeros((SUBLANES, d), F32).at[:n_batch].set(c).at[n_batch].set(c_ctx)
    mods = _ada_call(cond, ada_w, ada_b).reshape(depth, SUBLANES, N_ADA, 1, d)
    rw_pad = jnp.pad(router_w, ((0, 0), (0, 0), (0, LANES - N_EXPERTS)))
    rb_pad = jnp.pad(router_b, ((0, 0), (0, LANES - N_EXPERTS))).reshape(depth, 1, LANES)
    cos_t, sin_t = _rope_tables(n_seq)
    for l in range(depth):
        need_ctx = l < depth - 1
        last = l == depth - 1
        mods_l = mods[l]
        p = _in_call(xf, norm1_w[l], mods_l, _permute_w_in(w_in, l), n_batch=n_batch, n_seq=n_seq)
        hy_params = (hy_conv_w[l], hy_conv_b[l], hy_filt_w1[l], hy_filt_b1[l], hy_filt_freq[l], hy_filt_w2[l],
                     hy_filt_b2[l], hy_filt_w3[l], hy_decay[l], hy_bias[l])
        y_a = _hyena_layer(p, hy_params, n_batch=n_batch, seq_len=n_seq, row0=0)
        kvp = _kv_prep_call(p, cos_t, sin_t, n_batch=n_batch, n_seq=n_seq)
        y_b = _attn_call(attn_sink[l], p, kvp, cos_t, sin_t, n_batch=n_batch, n_seq=n_seq, n_ctx=n_ctx,
                         q_row0=0, ctx_row0=t_lat, has_local=True)
        y_c, y_cc = _gla_layer(p, gla_a_w[l], gla_a_b[l], gla_norm_w[l], n_batch=n_batch, n_seq=n_seq,
                               n_ctx=n_ctx, need_ctx=need_ctx)
        ys_ctx = None
        if need_ctx:
            y_ac = _hyena_layer(p, hy_params, n_batch=n_batch, seq_len=n_ctx, row0=t_lat)
            y_bc = _attn_call(attn_sink[l], p, kvp, None, None, n_batch=n_batch, n_seq=n_ctx, n_ctx=n_ctx,
                              q_row0=t_lat, ctx_row0=t_lat, has_local=False)
            ys_ctx = (y_ac, y_bc, y_cc)
        n_rows = xf.shape[0] if need_ctx else t_lat
        xf, h2p, logits = _merge_call(xf, p, mods_l, norm2_w[l], (y_a, y_b, y_c), ys_ctx,
                                      w_branch[l].astype(BF16), w_out[l].astype(BF16),
                                      rw_pad[l], rb_pad[l], n_batch=n_batch, n_seq=n_seq, n_rows=n_rows)
        xf = _moe_layer(xf, h2p, logits, mods_l, moe_w1, moe_b1, moe_w2, moe_b2, l, final_norm_w,
                        n_batch=n_batch, n_seq=n_seq, final_norm=last)
    return xf.reshape(n_batch, n_seq, d)
```

```python
import functools
import math

import jax
import jax.numpy as jnp
import numpy as np
from jax import lax
from jax.experimental import pallas as pl
from jax.experimental.pallas import tpu as pltpu

BF16 = jnp.bfloat16
F32 = jnp.float32

D_MODEL = 2048
DEPTH = 4
GRID_W = 64
NORM_EPS = 1e-5
N_ADA = 6
MIX_WIDTH = D_MODEL // 2

HY_WIDTH = MIX_WIDTH
HY_BANDS = 16
HY_EMB = 1 + 2 * HY_BANDS
HY_FFN = 64

ATT_HEAD_DIM = 64
ATT_HEADS = MIX_WIDTH // ATT_HEAD_DIM
ATT_KV_HEADS = 2
ATT_GROUP = ATT_HEADS // ATT_KV_HEADS
WINDOW = 128
ATT_BLOCK = 128
ROPE_BASE = 10000.0
NEG_INF = -1e30

GLA_HEADS = 4
GLA_DV = MIX_WIDTH // GLA_HEADS
GLA_DK = GLA_DV // 2
GLA_RANK = 16
GLA_TAU = 16.0
GLA_CHUNK = 64

N_BRANCH = 3
N_EXPERTS = 32
TOP_K = 4
EXPERT_FF = D_MODEL // 4
SWIGLU_LIMIT = 7.0
SWIGLU_ALPHA = 1.702

LANES = 128
SUBLANES = 8

COL_HY_X0 = 0
COL_HY_X1 = 1024
COL_HY_V = 2048
COL_ATT_Q = 3072
COL_GLA_V = 4096
COL_GLA_R = 5120
COL_GATES = 6144
COL_GLA_Q = 12288
COL_GLA_K = 12800
COL_ATT_K = 13312
COL_ATT_V = 13440
COL_DECAY = 13568
P_PAD = 13824


def _cparams(dims, vmem_mb=48):
    return pltpu.CompilerParams(dimension_semantics=dims, vmem_limit_bytes=vmem_mb << 20)


def _split_bf16(a):
    hi = a.astype(BF16)
    lo = (a - hi.astype(F32)).astype(BF16)
    return hi, lo


def _ada_kernel(c_ref, w_ref, b_ref, o_ref):
    cv = c_ref[...]
    s_hi, s_lo = _split_bf16(cv * jax.nn.sigmoid(cv))
    w_hi, w_lo = _split_bf16(w_ref[...])
    acc = jnp.dot(s_hi, w_hi, preferred_element_type=F32)
    acc += jnp.dot(s_lo, w_hi, preferred_element_type=F32)
    acc += jnp.dot(s_hi, w_lo, preferred_element_type=F32)
    o_ref[...] = acc + b_ref[...]


def _ada_call(cond8, ada_w, ada_b):
    depth, d, n6 = ada_w.shape
    tn = 1024
    return pl.pallas_call(
        _ada_kernel,
        grid=(depth, n6 // tn),
        in_specs=[
            pl.BlockSpec((SUBLANES, d), lambda l, j: (0, 0)),
            pl.BlockSpec((None, d, tn), lambda l, j: (l, 0, j)),
            pl.BlockSpec((None, 1, tn), lambda l, j: (l, 0, j)),
        ],
        out_specs=pl.BlockSpec((None, SUBLANES, tn), lambda l, j: (l, 0, j)),
        out_shape=jax.ShapeDtypeStruct((depth, SUBLANES, n6), F32),
        compiler_params=_cparams(("arbitrary", "arbitrary")),
        name="ada",
    )(cond8, ada_w, ada_b.reshape(depth, 1, n6))


def _rms_modulate(xf, nw, sh, sc1):
    ms = jnp.mean(xf * xf, axis=-1, keepdims=True)
    return (xf * lax.rsqrt(ms + NORM_EPS) * nw) * sc1 + sh


def _in_kernel(x_ref, nw_ref, sh_ref, sc_ref, w_ref, o_ref, h_ref, *, tm, rc):
    @pl.when(pl.program_id(1) == 0)
    def _():
        nw = nw_ref[...]
        sh = sh_ref[...]
        sc1 = 1.0 + sc_ref[...]

        def body(r, carry):
            sl = pl.ds(pl.multiple_of(r * rc, rc), rc)
            h_ref[sl, :] = _rms_modulate(x_ref[sl, :], nw, sh, sc1).astype(BF16)
            return carry

        lax.fori_loop(0, tm // rc, body, 0)

    o_ref[...] = jnp.dot(h_ref[...], w_ref[...], preferred_element_type=F32).astype(o_ref.dtype)


def _token_tile(n_lat_per_batch, n_ctx_total, cap):
    tm = cap
    while n_lat_per_batch % tm or n_ctx_total % tm:
        tm //= 2
    return tm


def _mod_row(i, tiles_per_batch, n_lat_tiles, n_batch):
    return jnp.where(i < n_lat_tiles, i // tiles_per_batch, n_batch)


def _in_call(xf, norm_w, mods_l, w_bf, *, n_batch, n_seq):
    t_rows, d = xf.shape
    p_pad = w_bf.shape[1]
    tm = _token_tile(n_seq, t_rows - n_batch * n_seq, 1024)
    tn = 1536
    tpb = n_seq // tm
    nlt = n_batch * tpb

    def mod_spec(comp):
        return pl.BlockSpec((None, None, 1, d), lambda i, j: (_mod_row(i, tpb, nlt, n_batch), comp, 0, 0))

    return pl.pallas_call(
        functools.partial(_in_kernel, tm=tm, rc=min(tm, 64)),
        grid=(t_rows // tm, p_pad // tn),
        in_specs=[
            pl.BlockSpec((tm, d), lambda i, j: (i, 0)),
            pl.BlockSpec((1, d), lambda i, j: (0, 0)),
            mod_spec(0),
            mod_spec(1),
            pl.BlockSpec((d, tn), lambda i, j: (0, j)),
        ],
        out_specs=pl.BlockSpec((tm, tn), lambda i, j: (i, j)),
        out_shape=jax.ShapeDtypeStruct((t_rows, p_pad), BF16),
        scratch_shapes=[pltpu.VMEM((tm, d), BF16)],
        compiler_params=_cparams(("arbitrary", "arbitrary")),
        name="in_proj",
    )(xf, norm_w.reshape(1, d), mods_l, mods_l, w_bf)


def _rope_tables(n_tokens):
    pos = np.arange(n_tokens)
    row_pos = (pos // GRID_W).astype(np.float32)
    col_pos = (pos % GRID_W).astype(np.float32)
    n_freq = ATT_HEAD_DIM // 4
    inv = (ROPE_BASE ** (-np.arange(n_freq, dtype=np.float32) / n_freq)).astype(np.float32)
    lane = np.arange(LANES)
    dd = lane % ATT_HEAD_DIM
    axis = dd // (2 * n_freq)
    freq = dd % n_freq
    second = (dd % (2 * n_freq)) >= n_freq
    p = jnp.where(jnp.asarray(axis)[None, :] == 0, jnp.asarray(row_pos)[:, None], jnp.asarray(col_pos)[:, None])
    ang = p * jnp.asarray(inv)[jnp.asarray(freq)][None, :]
    sign = jnp.where(jnp.asarray(second), 1.0, -1.0).astype(F32)
    return jnp.cos(ang), jnp.sin(ang) * sign[None, :]


def _rope_tile(xf, cos, sin_signed):
    lane = lax.broadcasted_iota(jnp.int32, xf.shape, 1)
    first = (lane % 32) < 16
    partner = jnp.where(first, pltpu.roll(xf, LANES - 16, 1), pltpu.roll(xf, 16, 1))
    return xf * cos + partner * sin_signed


KV_SLOTS = 4


def _kv_prep_kernel(k_ref, v_ref, cos_ref, sin_ref, o_ref, *, n_lat_tiles):
    kf = k_ref[...].astype(F32)
    kf = jnp.where(pl.program_id(0) < n_lat_tiles, _rope_tile(kf, cos_ref[...], sin_ref[...]), kf)
    vf = v_ref[...].astype(F32)
    half = LANES // 2
    o_ref[:, 0:LANES] = kf.astype(BF16)
    o_ref[:, LANES:2 * LANES] = pltpu.roll(kf, half, 1).astype(BF16)
    o_ref[:, 2 * LANES:3 * LANES] = vf.astype(BF16)
    o_ref[:, 3 * LANES:4 * LANES] = pltpu.roll(vf, half, 1).astype(BF16)


def _kv_prep_call(p, cos_t, sin_t, *, n_batch, n_seq):
    t_rows = p.shape[0]
    tr = _token_tile(n_seq, t_rows - n_batch * n_seq, 256)
    nb = n_seq // tr
    nlt = n_batch * nb
    tab = pl.BlockSpec((tr, LANES), lambda i: (jnp.where(i < nlt, i % nb, 0), 0))
    return pl.pallas_call(
        functools.partial(_kv_prep_kernel, n_lat_tiles=nlt),
        grid=(t_rows // tr,),
        in_specs=[pl.BlockSpec((tr, LANES), lambda i: (i, COL_ATT_K // LANES)),
                  pl.BlockSpec((tr, LANES), lambda i: (i, COL_ATT_V // LANES)), tab, tab],
        out_specs=pl.BlockSpec((tr, KV_SLOTS * LANES), lambda i: (i, 0)),
        out_shape=jax.ShapeDtypeStruct((t_rows, KV_SLOTS * LANES), BF16),
        compiler_params=_cparams(("arbitrary",)),
        name="kv_prep",
    )(p, p, cos_t, sin_t)


def _attn_kernel(*refs, has_local, n_seq):
    if has_local:
        sink_ref, q_ref, kvp_ref, kvc_ref, kvn_ref, kvx_ref, cos_ref, sin_ref, o_ref = refs
    else:
        sink_ref, q_ref, kvx_ref, o_ref = refs
    i = pl.program_id(1)
    npair = ATT_GROUP // 2
    rows = npair * ATT_BLOCK
    n_loc = 3 * ATT_BLOCK if has_local else 0
    if has_local:
        kv_rows = jnp.concatenate([kvp_ref[...], kvc_ref[...], kvn_ref[...], kvx_ref[...]], axis=0)
        r_i = lax.broadcasted_iota(jnp.int32, (rows, n_loc), 0)
        c_i = lax.broadcasted_iota(jnp.int32, (rows, n_loc), 1)
        qpos = i * ATT_BLOCK + (r_i & (ATT_BLOCK - 1))
        kpos = (i - 1) * ATT_BLOCK + c_i
        allowed = (kpos >= 0) & (kpos < n_seq) & (jnp.abs(qpos - kpos) <= WINDOW)
        cos = cos_ref[...]
        sin = sin_ref[...]
    else:
        kv_rows = kvx_ref[...]
    n_keys = kv_rows.shape[0]
    low = lax.broadcasted_iota(jnp.int32, (n_keys, LANES), 1) < ATT_HEAD_DIM
    out_low = lax.broadcasted_iota(jnp.int32, (rows, LANES), 1) < ATT_HEAD_DIM
    pair_of_row = lax.broadcasted_iota(jnp.int32, (rows, 1), 0) // ATT_BLOCK
    zero = jnp.zeros((n_keys, LANES), BF16)
    nt = (((1,), (1,)), ((), ()))

    def slot(s):
        return kv_rows[:, s * LANES:(s + 1) * LANES]

    for g in range(ATT_KV_HEADS):
        ka, kb, va, vb = (slot(0), slot(1), slot(2), slot(3)) if g == 0 else (slot(1), slot(0), slot(3), slot(2))
        k2 = jnp.concatenate([jnp.where(low, ka, zero), jnp.where(low, zero, kb)], axis=0)
        v2 = jnp.concatenate([jnp.where(low, va, zero), jnp.where(low, zero, vb)], axis=0)
        qs = []
        for pr in range(npair):
            hp = g * npair + pr
            qf = q_ref[:, hp * LANES:(hp + 1) * LANES].astype(F32)
            if has_local:
                qf = _rope_tile(qf, cos, sin)
            qs.append((qf * (ATT_HEAD_DIM ** -0.5)).astype(BF16))
        s = lax.dot_general(jnp.concatenate(qs, axis=0), k2, nt, preferred_element_type=F32)
        probs, dens = [], []
        for hf in range(2):
            sink = jnp.zeros((rows, 1), F32)
            for pr in range(npair):
                sink = jnp.where(pair_of_row == pr, sink_ref[(g * npair + pr) * 2 + hf], sink)
            sh = s[:, hf * n_keys:(hf + 1) * n_keys]
            s_x = sh[:, n_loc:]
            m = jnp.maximum(jnp.max(s_x, axis=-1, keepdims=True), sink)
            if has_local:
                s_l = jnp.where(allowed, sh[:, :n_loc], NEG_INF)
                m = jnp.maximum(m, jnp.max(s_l, axis=-1, keepdims=True))
                e_l = jnp.exp(s_l - m)
                probs.append(e_l.astype(BF16))
            e_x = jnp.exp(s_x - m)
            probs.append(e_x.astype(BF16))
            den = jnp.sum(e_x, axis=-1, keepdims=True) + jnp.exp(sink - m)
            if has_local:
                den = den + jnp.sum(e_l, axis=-1, keepdims=True)
            dens.append(den)
        o = jnp.dot(jnp.concatenate(probs, axis=1), v2, preferred_element_type=F32)
        o = o * jnp.where(out_low, 1.0 / dens[0], 1.0 / dens[1])
        for pr in range(npair):
            hp = g * npair + pr
            o_ref[:, hp * LANES:(hp + 1) * LANES] = o[pr * ATT_BLOCK:(pr + 1) * ATT_BLOCK].astype(o_ref.dtype)


def _attn_call(sink, p, kvp, cos_t, sin_t, *, n_batch, n_seq, n_ctx, q_row0, ctx_row0, has_local):
    nq = n_seq // ATT_BLOCK
    qb0 = q_row0 // ATT_BLOCK
    cb0 = ctx_row0 // n_ctx
    kvw = KV_SLOTS * LANES

    def loc(d):
        return pl.BlockSpec((ATT_BLOCK, kvw), lambda b, i: (qb0 + b * nq + jnp.clip(i + d, 0, nq - 1), 0))

    in_specs = [pl.BlockSpec(memory_space=pltpu.SMEM),
                pl.BlockSpec((ATT_BLOCK, MIX_WIDTH), lambda b, i: (qb0 + b * nq + i, COL_ATT_Q // MIX_WIDTH))]
    args = [sink, p]
    if has_local:
        in_specs += [loc(-1), loc(0), loc(1)]
        args += [kvp, kvp, kvp]
    in_specs += [pl.BlockSpec((n_ctx, kvw), lambda b, i: (cb0 + b, 0))]
    args += [kvp]
    if has_local:
        in_specs += [pl.BlockSpec((ATT_BLOCK, LANES), lambda b, i: (i, 0)),
                     pl.BlockSpec((ATT_BLOCK, LANES), lambda b, i: (i, 0))]
        args += [cos_t, sin_t]
    return pl.pallas_call(
        functools.partial(_attn_kernel, has_local=has_local, n_seq=n_seq),
        grid=(n_batch, nq),
        in_specs=in_specs,
        out_specs=pl.BlockSpec((ATT_BLOCK, MIX_WIDTH), lambda b, i: (b * nq + i, 0)),
        out_shape=jax.ShapeDtypeStruct((n_batch * n_seq, MIX_WIDTH), BF16),
        compiler_params=_cparams(("arbitrary", "arbitrary")),
        name="attn_local" if has_local else "attn_ctx",
    )(*args)


def _gla_kernel(*refs, reverse, fuse_out, tb):
    if fuse_out:
        (q_ref, k_ref, v_ref, pa_ref, aw_ref, ab_ref, s0_ref, oo_ref, r_ref, nw_ref,
         y_ref, sout_ref, st_ref, la_ref) = refs
    else:
        q_ref, k_ref, v_ref, pa_ref, aw_ref, ab_ref, s0_ref, y_ref, sout_ref, st_ref, la_ref = refs
    i = pl.program_id(1)
    ck = GLA_CHUNK
    nchunk = tb // ck

    @pl.when(i == 0)
    def _():
        st_ref[...] = s0_ref[...]

    z = jnp.dot(pa_ref[...], aw_ref[...], preferred_element_type=F32) + ab_ref[...]
    la_ref[...] = (jnp.minimum(z, 0.0) - jnp.log(1.0 + jnp.exp(-jnp.abs(z)))) * (1.0 / GLA_TAU)

    r_i = lax.broadcasted_iota(jnp.int32, (ck, ck), 0)
    c_i = lax.broadcasted_iota(jnp.int32, (ck, ck), 1)
    tri = (c_i >= r_i) if reverse else (c_i <= r_i)
    tri_b = jnp.where(tri, 1.0, 0.0).astype(BF16)
    nt = (((1,), (1,)), ((), ()))
    tn = (((0,), (0,)), ((), ()))
    if fuse_out:
        nw = nw_ref[...]

    def chunk(ci):
        c = (nchunk - 1 - ci) if reverse else ci
        rs = pl.ds(pl.multiple_of(c * ck, ck), ck)
        la_hi, la_lo = _split_bf16(la_ref[rs, :])
        b = jnp.dot(tri_b, la_hi, preferred_element_type=F32) + jnp.dot(tri_b, la_lo, preferred_element_type=F32)
        b_tot = b[0:1, :] if reverse else b[ck - 1:ck, :]
        kf = k_ref[rs, :].astype(F32)
        q_in = (q_ref[rs, :].astype(F32) * (GLA_DK ** -0.5) * jnp.exp(b)).astype(BF16)
        k_in = (kf * jnp.exp(-b)).astype(BF16)
        k_st = (kf * jnp.exp(b_tot - b)).astype(BF16)
        dec = jnp.exp(b_tot)
        vv = v_ref[rs, :]
        for h in range(GLA_HEADS):
            hs = slice(h * GLA_DK, (h + 1) * GLA_DK)
            vs = slice(h * GLA_DV, (h + 1) * GLA_DV)
            att = lax.dot_general(q_in[:, hs], k_in[:, hs], nt, preferred_element_type=F32)
            att = jnp.where(tri, att, 0.0).astype(BF16)
            st = st_ref[h]
            o = jnp.dot(att, vv[:, vs], preferred_element_type=F32)
            o = o + lax.dot_general(q_in[:, hs], st.astype(BF16), nt, preferred_element_type=F32)
            st_ref[h] = st * dec[:, hs] + lax.dot_general(vv[:, vs], k_st[:, hs], tn, preferred_element_type=F32)
            if fuse_out:
                tot = o + oo_ref[rs, vs]
                yn = tot * lax.rsqrt(jnp.mean(tot * tot, axis=-1, keepdims=True) + NORM_EPS) * nw
                rf = r_ref[rs, vs].astype(F32)
                y_ref[rs, vs] = (yn * (rf * jax.nn.sigmoid(rf))).astype(y_ref.dtype)
            else:
                y_ref[rs, vs] = o

    per_iter = 2 if nchunk % 2 == 0 else 1

    def body(it, carry):
        for u in range(per_iter):
            chunk(it * per_iter + u)
        return carry

    lax.fori_loop(0, nchunk // per_iter, body, 0)

    @pl.when(i == pl.num_programs(1) - 1)
    def _():
        sout_ref[...] = st_ref[...]


def _gla_call(p, aw_pad, ab, s0, o_other, norm_w, *, n_batch, n_seq, row0, reverse):
    fuse_out = o_other is not None
    tb = min(n_seq, 512)
    nblk = n_seq // tb
    rb0 = row0 // tb

    def blk(i):
        return (nblk - 1 - i) if reverse else i

    def prow(width, col):
        return pl.BlockSpec((tb, width), lambda b, i: (rb0 + b * nblk + blk(i), col // width))

    def orow(width):
        return pl.BlockSpec((tb, width), lambda b, i: (b * nblk + blk(i), 0))

    st_spec = pl.BlockSpec((None, GLA_HEADS, GLA_DV, GLA_DK), lambda b, i: (b, 0, 0, 0))
    nq = GLA_HEADS * GLA_DK
    in_specs = [prow(nq, COL_GLA_Q), prow(nq, COL_GLA_K), prow(MIX_WIDTH, COL_GLA_V), prow(LANES, COL_DECAY),
                pl.BlockSpec((LANES, nq), lambda b, i: (0, 0)), pl.BlockSpec((1, nq), lambda b, i: (0, 0)), st_spec]
    args = [p, p, p, p, aw_pad, ab, s0]
    if fuse_out:
        in_specs += [orow(MIX_WIDTH), prow(MIX_WIDTH, COL_GLA_R), pl.BlockSpec((1, GLA_DV), lambda b, i: (0, 0))]
        args += [o_other, p, norm_w.reshape(1, GLA_DV)]
    return pl.pallas_call(
        functools.partial(_gla_kernel, reverse=reverse, fuse_out=fuse_out, tb=tb),
        grid=(n_batch, nblk),
        in_specs=in_specs,
        out_specs=[orow(MIX_WIDTH), st_spec],
        out_shape=[jax.ShapeDtypeStruct((n_batch * n_seq, MIX_WIDTH), BF16 if fuse_out else F32),
                   jax.ShapeDtypeStruct((n_batch, GLA_HEADS, GLA_DV, GLA_DK), F32)],
        scratch_shapes=[pltpu.VMEM((GLA_HEADS, GLA_DV, GLA_DK), F32), pltpu.VMEM((tb, nq), F32)],
        compiler_params=_cparams(("arbitrary", "arbitrary")),
        name="gla_" + ("bwd" if reverse else "fwd") + ("_out" if fuse_out else ""),
    )(*args)


def _gla_decay_weights(a_w, a_b):
    nq = GLA_HEADS * GLA_DK
    pads = []
    for d in range(2):
        z = jnp.zeros((LANES, nq), F32).at[d * GLA_RANK:(d + 1) * GLA_RANK].set(a_w[d])
        pads.append(z.astype(BF16))
    return pads, [a_b[0].reshape(1, nq), a_b[1].reshape(1, nq)]


def _gla_layer(p, a_w, a_b, norm_w, *, n_batch, n_seq, n_ctx, need_ctx):
    (aw_f, aw_b), (ab_f, ab_b) = _gla_decay_weights(a_w, a_b)
    t_lat = n_batch * n_seq
    zero = jnp.zeros((n_batch, GLA_HEADS, GLA_DV, GLA_DK), F32)
    kw_c = dict(n_batch=n_batch, n_seq=n_ctx, row0=t_lat)
    kw_l = dict(n_batch=n_batch, n_seq=n_seq, row0=0)
    oc_b, s_b = _gla_call(p, aw_b, ab_b, zero, None, None, reverse=True, **kw_c)
    if need_ctx:
        y_cc, s_f = _gla_call(p, aw_f, ab_f, zero, oc_b, norm_w, reverse=False, **kw_c)
    else:
        y_cc = None
        _, s_f = _gla_call(p, aw_f, ab_f, zero, None, None, reverse=False, **kw_c)
    ol_b, _ = _gla_call(p, aw_b, ab_b, s_b, None, None, reverse=True, **kw_l)
    y_c, _ = _gla_call(p, aw_f, ab_f, s_f, ol_b, norm_w, reverse=False, **kw_l)
    return y_c, y_cc


HY_N2 = 128
HY_J = SUBLANES
HY_G = HY_N2 // HY_J
HY_CB = 128
HY_TILE = 2 * HY_J


def _dot3(a, b):
    a_hi, a_lo = _split_bf16(a)
    b_hi, b_lo = _split_bf16(b)
    acc = jnp.dot(a_hi, b_hi, preferred_element_type=F32)
    acc += jnp.dot(a_lo, b_hi, preferred_element_type=F32)
    acc += jnp.dot(a_hi, b_lo, preferred_element_type=F32)
    return acc


def _hy_filter_kernel(z_ref, w1_ref, b1_ref, fr_ref, w2_ref, b2_ref, w3_ref, dec_ref, o_ref):
    z = z_ref[...]
    fr = fr_ref[...]
    h = jnp.sin(fr * (_dot3(z, w1_ref[...]) + b1_ref[...]))
    h = jnp.sin(fr * (_dot3(h, w2_ref[...]) + b2_ref[...]))
    t = z[:, 0:1]
    o_ref[...] = _dot3(h, w3_ref[...]) * jnp.exp(-t * jnp.abs(dec_ref[...]))


def _hy_filter_call(seq_len, w1, b1, freq, w2, b2, w3, decay):
    pos = np.arange(seq_len, dtype=np.float32)
    t = pos / np.float32(max(seq_len - 1, 1))
    ang = (np.float32(2.0 * math.pi) * pos / np.float32(seq_len)).astype(np.float32)
    bands = np.linspace(1e-4, HY_BANDS - 1, HY_BANDS, dtype=np.float32)
    fw = (ang[:, None] * bands[None, :]).astype(np.float32)
    z = jnp.concatenate([jnp.asarray(t)[:, None], jnp.cos(jnp.asarray(fw)), -jnp.sin(jnp.asarray(fw))], axis=-1)
    z = jnp.pad(z, ((0, 0), (0, LANES - HY_EMB)))
    padh = LANES - HY_FFN
    w1p = jnp.pad(w1, ((0, LANES - HY_EMB), (0, padh)))
    w2p = jnp.pad(w2, ((0, padh), (0, padh)))
    w3p = jnp.pad(w3, ((0, padh), (0, 0)))
    row = lambda v: jnp.pad(v, (0, padh)).reshape(1, LANES)
    n_out = w3.shape[1]
    tl = min(seq_len, 256)
    full = lambda shape: pl.BlockSpec(shape, lambda i: (0, 0))
    return pl.pallas_call(
        _hy_filter_kernel,
        grid=(seq_len // tl,),
        in_specs=[pl.BlockSpec((tl, LANES), lambda i: (i, 0)), full((LANES, LANES)), full((1, LANES)),
                  full((1, LANES)), full((LANES, LANES)), full((1, LANES)), full((LANES, n_out)), full((1, n_out))],
        out_specs=pl.BlockSpec((tl, n_out), lambda i: (i, 0)),
        out_shape=jax.ShapeDtypeStruct((seq_len, n_out), F32),
        compiler_params=_cparams(("arbitrary",)),
        name="hy_filter",
    )(z, w1p, row(b1), row(freq), w2p, row(b2), w3p, decay.reshape(1, n_out))


@functools.lru_cache(maxsize=None)
def _hy_constants(seq_len):
    nfft = 2 * seq_len
    n2, jj, g = HY_N2, HY_J, HY_G
    n1 = nfft // n2
    h1 = n1 // 2
    k1 = np.arange(n1)

    def kron_mat(rows_n, cols_n, sign, both_parts_in):
        m = np.zeros((rows_n, 2, jj, cols_n, 2 if both_parts_in else 1, jj))
        for j in range(jj):
            r = np.arange(rows_n)[:, None]
            c = np.arange(cols_n)[None, :]
            kk = r if sign < 0 else c
            ph = sign * 2.0 * np.pi * (r * c / n1 + kk * j / nfft)
            cr, ci = np.cos(ph), np.sin(ph)
            m[:, 0, j, :, 0, j] = cr
            m[:, 1, j, :, 0, j] = ci
            if both_parts_in:
                m[:, 0, j, :, 1, j] = -ci
                m[:, 1, j, :, 1, j] = cr
        return m.reshape(rows_n * 2 * jj, -1)

    ma = kron_mat(n1, h1, -1.0, True)
    mak = kron_mat(n1, h1, -1.0, False)
    mai = kron_mat(h1, n1, +1.0, True)

    nn = (np.arange(g)[:, None] * jj + np.arange(jj)[None, :])
    kk2 = np.arange(n2)
    ph = -2.0 * np.pi * kk2[:, None, None] * nn[None] / n2
    mc = np.zeros((2, n2, g, 2, jj))
    mc[0, :, :, 0, :] = np.cos(ph)
    mc[0, :, :, 1, :] = -np.sin(ph)
    mc[1, :, :, 0, :] = np.sin(ph)
    mc[1, :, :, 1, :] = np.cos(ph)
    mc = mc.reshape(2 * n2, 2 * n2)
    mci = np.zeros((g, 2, jj, 2, n2))
    phi = np.transpose(-ph, (1, 2, 0))
    mci[:, 0, :, 0, :] = np.cos(phi)
    mci[:, 0, :, 1, :] = -np.sin(phi)
    mci[:, 1, :, 0, :] = np.sin(phi)
    mci[:, 1, :, 1, :] = np.cos(phi)
    mci = mci.reshape(2 * n2, 2 * n2)

    th = 2.0 * np.pi * (np.arange(g)[:, None] * jj) * k1[None, :] / nfft
    twc = np.cos(th).astype(np.float32).reshape(-1)
    tws = (-np.sin(th)).astype(np.float32).reshape(-1)
    to_bf = lambda m: m.astype(np.float32).astype(BF16)
    return dict(ma=to_bf(ma), mak=to_bf(mak), mai=to_bf(mai), mc=to_bf(mc), mci=to_bf(mci), twc=twc, tws=tws,
                n1=n1, h1=h1)


def _hy_twiddle_tiles(y, g, twc_ref, tws_ref, n1, conj):
    tiles = []
    for k1 in range(n1):
        c = twc_ref[g * n1 + k1]
        s = tws_ref[g * n1 + k1]
        if conj:
            s = -s
        yr = y[HY_TILE * k1:HY_TILE * k1 + HY_J]
        yi = y[HY_TILE * k1 + HY_J:HY_TILE * (k1 + 1)]
        tiles.append(jnp.concatenate([c * yr - s * yi, s * yr + c * yi], axis=0).astype(BF16))
    return tiles


def _hy_tile_rows(k1, g):
    return pl.ds(pl.multiple_of(k1 * 2 * HY_N2 + g * HY_TILE, HY_TILE), HY_TILE)


def _hy_khat_kernel(twc_ref, tws_ref, hf_ref, hb_ref, mak_ref, mc_ref, kh_ref, a_ref, *, n1):
    nfft = n1 * HY_N2
    h1 = n1 // 2
    cb = hf_ref.shape[1]

    def stage_a(g, carry):
        def slab(ref, r):
            return ref[pl.ds(pl.multiple_of(r * HY_N2 + g * HY_J, HY_J), HY_J), :]

        fs = [slab(hf_ref, r) for r in range(h1)]
        bs = [slab(hb_ref, r) for r in range(h1)]
        first = (lax.broadcasted_iota(jnp.int32, (HY_J, cb), 0) == 0) & (g == 0)
        bs[0] = jnp.where(first, 0.0, bs[0])
        rhs = jnp.concatenate([jnp.concatenate(fs, axis=0), jnp.concatenate(bs, axis=0)], axis=1).astype(BF16)
        y = jnp.dot(mak_ref[...], rhs, preferred_element_type=F32)
        for k1, t in enumerate(_hy_twiddle_tiles(y, g, twc_ref, tws_ref, n1, False)):
            a_ref[_hy_tile_rows(k1, g), :] = t
        return carry

    lax.fori_loop(0, HY_G, stage_a, 0)

    kc = min(4, n1)

    def stage_c(kq, carry):
        blks = [pl.ds(pl.multiple_of((kq * kc + u) * 2 * HY_N2, 2 * HY_N2), 2 * HY_N2) for u in range(kc)]
        x = jnp.dot(mc_ref[...], jnp.concatenate([a_ref[b, :] for b in blks], axis=1),
                    preferred_element_type=F32) * (1.0 / nfft)
        for u, b in enumerate(blks):
            xf, xb = x[:, 2 * u * cb:(2 * u + 1) * cb], x[:, (2 * u + 1) * cb:(2 * u + 2) * cb]
            kh_ref[b, :] = jnp.concatenate([xf[:HY_N2] + xb[:HY_N2], xf[HY_N2:] - xb[HY_N2:]], axis=0).astype(BF16)
        return carry

    lax.fori_loop(0, n1 // kc, stage_c, 0)


def _hy_khat_call(h, consts):
    seq_len, c2 = h.shape
    c = c2 // 2
    n1 = consts["n1"]
    rows = n1 * 2 * HY_N2
    ncb = c // HY_CB
    return pl.pallas_call(
        functools.partial(_hy_khat_kernel, n1=n1),
        grid=(ncb,),
        in_specs=[pl.BlockSpec(memory_space=pltpu.SMEM), pl.BlockSpec(memory_space=pltpu.SMEM),
                  pl.BlockSpec((seq_len, HY_CB), lambda i: (0, i)),
                  pl.BlockSpec((seq_len, HY_CB), lambda i: (0, ncb + i)),
                  pl.BlockSpec(consts["mak"].shape, lambda i: (0, 0)),
                  pl.BlockSpec(consts["mc"].shape, lambda i: (0, 0))],
        out_specs=pl.BlockSpec((rows, HY_CB), lambda i: (0, i)),
        out_shape=jax.ShapeDtypeStruct((rows, c), BF16),
        scratch_shapes=[pltpu.VMEM((rows, 2 * HY_CB), BF16)],
        compiler_params=_cparams(("arbitrary",)),
        name="hy_khat",
    )(jnp.asarray(consts["twc"]), jnp.asarray(consts["tws"]), h, h, jnp.asarray(consts["mak"]),
      jnp.asarray(consts["mc"]))


def _hy_conv3_chunk(ref, base, w, b, r, n_chunks, seq_len):
    r0 = pl.multiple_of(r * HY_N2, HY_N2)
    cur = ref[pl.ds(base + r0, HY_N2), :].astype(F32)
    pr0 = pl.multiple_of(jnp.maximum(r0 - HY_TILE, 0), HY_TILE)
    nx0 = pl.multiple_of(jnp.minimum(r0 + HY_N2, seq_len - HY_TILE), HY_TILE)
    prev_row = ref[pl.ds(base + pr0, HY_TILE), :].astype(F32)[HY_TILE - 1:HY_TILE]
    next_row = ref[pl.ds(base + nx0, HY_TILE), :].astype(F32)[0:1]
    prev_row = jnp.where(r > 0, prev_row, 0.0)
    next_row = jnp.where(r < n_chunks - 1, next_row, 0.0)
    row = lax.broadcasted_iota(jnp.int32, cur.shape, 0)
    up = jnp.where(row == 0, prev_row, pltpu.roll(cur, 1, 0))
    dn = jnp.where(row == HY_N2 - 1, next_row, pltpu.roll(cur, HY_N2 - 1, 0))
    return up * w[0:1] + cur * w[1:2] + dn * w[2:3] + b


def _hy_interleave(a, b):
    parts = []
    for g in range(HY_G):
        parts.append(a[g * HY_J:(g + 1) * HY_J])
        parts.append(b[g * HY_J:(g + 1) * HY_J])
    return jnp.concatenate(parts, axis=0)


def _hy_conv_kernel(twc_ref, tws_ref, x0_ref, x1_ref, v_ref,
                    cw0_ref, cw1_ref, cw2_ref, cb0_ref, cb1_ref, cb2_ref, bias_ref, kh_ref,
                    ma_ref, mc_ref, mci_ref, mai_ref, o_ref, zc_ref, x0s_ref, a_ref, *, n1, seq_len):
    h1 = n1 // 2
    tile_rows = 2 * HY_N2

    def prologue(r, carry):
        cw0, cw1, cw2 = cw0_ref[...], cw1_ref[...], cw2_ref[...]
        cb0, cb1, cb2 = cb0_ref[...], cb1_ref[...], cb2_ref[...]
        zs, x0s = [], []
        for base in (0, seq_len):
            x0s.append(_hy_conv3_chunk(x0_ref, base, cw0, cb0, r, h1, seq_len))
            x1c = _hy_conv3_chunk(x1_ref, base, cw1, cb1, r, h1, seq_len)
            vc = _hy_conv3_chunk(v_ref, base, cw2, cb2, r, h1, seq_len)
            zs.append(x1c * vc)
        dst = pl.ds(pl.multiple_of(r * tile_rows, tile_rows), tile_rows)
        zc_ref[dst, :] = _hy_interleave(zs[0], zs[1]).astype(BF16)
        x0s_ref[dst, :] = _hy_interleave(x0s[0], x0s[1]).astype(BF16)
        return carry

    lax.fori_loop(0, h1, prologue, 0)

    cb = o_ref.shape[1]
    ga = 2
    kc = min(4, n1)

    def stage_a(gp, carry):
        groups = [gp * ga + u for u in range(ga)]
        rhs = jnp.concatenate(
            [jnp.concatenate([zc_ref[_hy_tile_rows(r, g), :] for r in range(h1)], axis=0) for g in groups], axis=1)
        y = jnp.dot(ma_ref[...], rhs, preferred_element_type=F32)
        for u, g in enumerate(groups):
            for k1, t in enumerate(_hy_twiddle_tiles(y[:, u * cb:(u + 1) * cb], g, twc_ref, tws_ref, n1, False)):
                a_ref[_hy_tile_rows(k1, g), :] = t
        return carry

    lax.fori_loop(0, HY_G // ga, stage_a, 0)

    def stage_c(kq, carry):
        blks = [pl.ds(pl.multiple_of((kq * kc + u) * tile_rows, tile_rows), tile_rows) for u in range(kc)]
        x = jnp.dot(mc_ref[...], jnp.concatenate([a_ref[b, :] for b in blks], axis=1), preferred_element_type=F32)
        pms = []
        for u, b in enumerate(blks):
            kh = kh_ref[b, :].astype(F32)
            xr, xi = x[:HY_N2, u * cb:(u + 1) * cb], x[HY_N2:, u * cb:(u + 1) * cb]
            kr, ki = kh[:HY_N2], kh[HY_N2:]
            pms.append(jnp.concatenate([xr * kr - xi * ki, xr * ki + xi * kr], axis=0).astype(BF16))
        z = jnp.dot(mci_ref[...], jnp.concatenate(pms, axis=1), preferred_element_type=F32).astype(BF16)
        for u, b in enumerate(blks):
            a_ref[b, :] = z[:, u * cb:(u + 1) * cb]
        return carry

    lax.fori_loop(0, n1 // kc, stage_c, 0)

    bias = bias_ref[...]

    def stage_ai(gp, carry):
        groups = [gp * ga + u for u in range(ga)]
        cols = []
        for g in groups:
            tiles = []
            for k1 in range(n1):
                t = a_ref[_hy_tile_rows(k1, g), :].astype(F32)
                c = twc_ref[g * n1 + k1]
                s = -tws_ref[g * n1 + k1]
                tr, ti = t[:HY_J], t[HY_J:]
                tiles.append(jnp.concatenate([c * tr - s * ti, s * tr + c * ti], axis=0).astype(BF16))
            cols.append(jnp.concatenate(tiles, axis=0))
        y = jnp.dot(mai_ref[...], jnp.concatenate(cols, axis=1), preferred_element_type=F32)
        for u, g in enumerate(groups):
            for r in range(h1):
                src = _hy_tile_rows(r, g)
                z = zc_ref[src, :].astype(F32)
                x0 = x0s_ref[src, :].astype(F32)
                o = x0 * (y[HY_TILE * r:HY_TILE * (r + 1), u * cb:(u + 1) * cb] + bias * z)
                dst = pl.ds(pl.multiple_of(r * HY_N2 + g * HY_J, HY_J), HY_J)
                o_ref[dst, :] = o[:HY_J]
                dst2 = pl.ds(pl.multiple_of(seq_len + r * HY_N2 + g * HY_J, HY_J), HY_J)
                o_ref[dst2, :] = o[HY_J:]
        return carry

    lax.fori_loop(0, HY_G // ga, stage_ai, 0)


def _hy_conv_call(p, khat, conv_w, conv_b, bias, consts, *, n_batch, seq_len, row0):
    c = HY_WIDTH
    n1 = consts["n1"]
    h1 = consts["h1"]
    rb0 = row0 // (2 * seq_len)
    ncb = c // HY_CB
    tile_rows = 2 * HY_N2

    def pin(col):
        return pl.BlockSpec((2 * seq_len, HY_CB), lambda cb, pi: (rb0 + pi, col // HY_CB + cb))

    def cws(stream):
        return pl.BlockSpec((3, HY_CB), lambda cb, pi: (0, stream * ncb + cb))

    def cbs(stream):
        return pl.BlockSpec((1, HY_CB), lambda cb, pi: (0, stream * ncb + cb))

    const = lambda a: pl.BlockSpec(a.shape, lambda cb, pi: (0, 0))
    mats = [jnp.asarray(consts[k]) for k in ("ma", "mc", "mci", "mai")]
    in_specs = [pl.BlockSpec(memory_space=pltpu.SMEM), pl.BlockSpec(memory_space=pltpu.SMEM),
                pin(COL_HY_X0), pin(COL_HY_X1), pin(COL_HY_V), cws(0), cws(1), cws(2), cbs(0), cbs(1), cbs(2),
                pl.BlockSpec((1, HY_CB), lambda cb, pi: (0, cb)),
                pl.BlockSpec((n1 * tile_rows, HY_CB), lambda cb, pi: (0, cb))] + [const(m) for m in mats]
    return pl.pallas_call(
        functools.partial(_hy_conv_kernel, n1=n1, seq_len=seq_len),
        grid=(ncb, n_batch // 2),
        in_specs=in_specs,
        out_specs=pl.BlockSpec((2 * seq_len, HY_CB), lambda cb, pi: (pi, cb)),
        out_shape=jax.ShapeDtypeStruct((n_batch * seq_len, c), F32),
        scratch_shapes=[pltpu.VMEM((h1 * tile_rows, HY_CB), BF16), pltpu.VMEM((h1 * tile_rows, HY_CB), BF16),
                        pltpu.VMEM((n1 * tile_rows, HY_CB), BF16)],
        compiler_params=_cparams(("arbitrary", "arbitrary"), 56),
        name="hy_conv",
    )(jnp.asarray(consts["twc"]), jnp.asarray(consts["tws"]), p, p, p,
      conv_w, conv_w, conv_w, conv_b.reshape(1, -1), conv_b.reshape(1, -1), conv_b.reshape(1, -1),
      bias.reshape(1, -1), khat, *mats)


def _hyena_layer(p, params, *, n_batch, seq_len, row0):
    conv_w, conv_b, w1, b1, freq, w2, b2, w3, decay, bias = params
    consts = _hy_constants(seq_len)
    h = _hy_filter_call(seq_len, w1, b1, freq, w2, b2, w3, decay)
    khat = _hy_khat_call(h, consts)
    return _hy_conv_call(p, khat, conv_w, conv_b, bias, consts, n_batch=n_batch, seq_len=seq_len, row0=row0)


ROW_WORDS = D_MODEL // 2
ROW_SUB = ROW_WORDS // LANES


def _pack_rows(val, ref2d, n_rows):
    lo = lax.bitcast_convert_type(val[:, :ROW_WORDS].astype(BF16).astype(F32), jnp.uint32)
    hi = lax.bitcast_convert_type(val[:, ROW_WORDS:].astype(BF16).astype(F32), jnp.uint32)
    word = (hi & jnp.uint32(0xFFFF0000)) | (lo >> 16)
    for s in range(ROW_SUB):
        ref2d[pl.ds(s, n_rows, stride=ROW_SUB), :] = word[:, s * LANES:(s + 1) * LANES]


def _unpack_words(w):
    lo = lax.bitcast_convert_type(w << 16, F32)
    hi = lax.bitcast_convert_type(w & jnp.uint32(0xFFFF0000), F32)
    return lo, hi


def _merge_kernel(*refs, has_ctx, n_lat_tiles, tm):
    if has_ctx:
        (x_ref, g1_ref, sh2_ref, sc2_ref, nw2_ref, gl0_ref, gl1_ref, gl2_ref, ya_l, yb_l, yc_l, ya_c, yb_c, yc_c,
         wb_ref, wo_ref, rw_ref, rb_ref, xo_ref, h2p_ref, lg_ref) = refs
    else:
        (x_ref, g1_ref, sh2_ref, sc2_ref, nw2_ref, gl0_ref, gl1_ref, gl2_ref, ya_l, yb_l, yc_l,
         wb_ref, wo_ref, rw_ref, rb_ref, xo_ref, h2p_ref, lg_ref) = refs
    ys = [ya_l[...].astype(BF16), yb_l[...], yc_l[...]]
    if has_ctx:
        is_ctx = pl.program_id(0) >= n_lat_tiles
        ys = [jnp.where(is_ctx, c, l) for c, l in zip((ya_c[...].astype(BF16), yb_c[...], yc_c[...]), ys)]
    m = None
    for g, gl_ref in enumerate((gl0_ref, gl1_ref, gl2_ref)):
        proj = jnp.dot(ys[g], wb_ref[g], preferred_element_type=F32)
        term = jax.nn.sigmoid(gl_ref[...].astype(F32)) * proj
        m = term if m is None else m + term
    out = jnp.dot(m.astype(BF16), wo_ref[...], preferred_element_type=F32)
    xn = x_ref[...] + g1_ref[...] * out
    xo_ref[...] = xn
    h2 = _rms_modulate(xn, nw2_ref[...], sh2_ref[...], 1.0 + sc2_ref[...])
    lg_ref[...] = _dot3(h2, rw_ref[...]) + rb_ref[...]
    _pack_rows(h2, h2p_ref, tm)


def _merge_call(xf, p, mods_l, norm2_w, ys_lat, ys_ctx, wb_bf, wo_bf, rw_pad, rb_pad, *, n_batch, n_seq, n_rows):
    d = D_MODEL
    has_ctx = ys_ctx is not None
    tm = 256
    tpb = n_seq // tm
    nlt = n_batch * tpb
    nt = n_rows // tm

    def mod_spec(comp):
        return pl.BlockSpec((None, None, 1, d), lambda i: (_mod_row(i, tpb, nlt, n_batch), comp, 0, 0))

    def gate_spec(g):
        return pl.BlockSpec((tm, d), lambda i: (i, COL_GATES // d + g))

    lat_spec = pl.BlockSpec((tm, MIX_WIDTH), lambda i: (jnp.minimum(i, nlt - 1), 0))
    ctx_spec = pl.BlockSpec((tm, MIX_WIDTH), lambda i: (jnp.maximum(i - nlt, 0), 0))
    once = pl.Buffered(1)
    in_specs = [pl.BlockSpec((tm, d), lambda i: (i, 0)), mod_spec(2), mod_spec(3), mod_spec(4),
                pl.BlockSpec((1, d), lambda i: (0, 0)), gate_spec(0), gate_spec(1), gate_spec(2),
                lat_spec, lat_spec, lat_spec]
    args = [xf, mods_l, mods_l, mods_l, norm2_w.reshape(1, d), p, p, p, *ys_lat]
    if has_ctx:
        in_specs += [ctx_spec, ctx_spec, ctx_spec]
        args += list(ys_ctx)
    in_specs += [pl.BlockSpec((N_BRANCH, MIX_WIDTH, d), lambda i: (0, 0, 0), pipeline_mode=once),
                 pl.BlockSpec((d, d), lambda i: (0, 0), pipeline_mode=once),
                 pl.BlockSpec((d, LANES), lambda i: (0, 0)), pl.BlockSpec((1, LANES), lambda i: (0, 0))]
    args += [wb_bf, wo_bf, rw_pad, rb_pad]
    return pl.pallas_call(
        functools.partial(_merge_kernel, has_ctx=has_ctx, n_lat_tiles=nlt, tm=tm),
        grid=(nt,),
        in_specs=in_specs,
        out_specs=[pl.BlockSpec((tm, d), lambda i: (i, 0)),
                   pl.BlockSpec((tm * ROW_SUB, LANES), lambda i: (i, 0)),
                   pl.BlockSpec((tm, LANES), lambda i: (i, 0))],
        out_shape=[jax.ShapeDtypeStruct((n_rows, d), F32),
                   jax.ShapeDtypeStruct((n_rows * ROW_SUB, LANES), jnp.uint32),
                   jax.ShapeDtypeStruct((n_rows, LANES), F32)],
        compiler_params=_cparams(("arbitrary",), 58),
        name="merge",
    )(*args)


def _route_kernel(lg_ref, idx_ref, rank_ref, gate_ref, cnt_ref, carry_ref, *, tr):
    i = pl.program_id(0)

    @pl.when(i == 0)
    def _():
        carry_ref[...] = jnp.zeros_like(carry_ref)

    lane = lax.broadcasted_iota(jnp.int32, (tr, LANES), 1)
    lg = jnp.where(lane < N_EXPERTS, lg_ref[...], -jnp.inf)
    tops, hots = [], []
    for _ in range(TOP_K):
        mx = jnp.max(lg, axis=-1, keepdims=True)
        sel = jnp.min(jnp.where(lg == mx, lane, LANES), axis=-1, keepdims=True)
        hot = lane == sel
        tops.append((mx, sel))
        hots.append(hot)
        lg = jnp.where(hot, -jnp.inf, lg)
    es = [jnp.exp(mx - tops[0][0]) for mx, _ in tops]
    den = es[0] + es[1] + es[2] + es[3]
    hot_f = [jnp.where(h, 1.0, 0.0) for h in hots]
    chosen = hot_f[0] + hot_f[1] + hot_f[2] + hot_f[3]
    r_i = lax.broadcasted_iota(jnp.int32, (tr, tr), 0)
    c_i = lax.broadcasted_iota(jnp.int32, (tr, tr), 1)
    below = jnp.where(c_i < r_i, 1.0, 0.0).astype(BF16)
    before = jnp.dot(below, chosen.astype(BF16), preferred_element_type=F32) + carry_ref[...]
    idx_o = jnp.zeros((tr, LANES), jnp.int32)
    rank_o = jnp.zeros((tr, LANES), jnp.int32)
    gate_o = jnp.zeros((tr, LANES), F32)
    for j in range(TOP_K):
        rank_j = jnp.sum(hot_f[j] * before, axis=-1, keepdims=True).astype(jnp.int32)
        idx_o = jnp.where(lane == j, tops[j][1], idx_o)
        rank_o = jnp.where(lane == j, rank_j, rank_o)
        gate_o = jnp.where(lane == j, es[j] / den, gate_o)
    idx_ref[...] = idx_o
    rank_ref[...] = rank_o
    gate_ref[...] = gate_o
    carry_ref[...] += jnp.sum(chosen, axis=0, keepdims=True)
    cnt_ref[...] = carry_ref[...]


def _route_call(logits):
    t_rows = logits.shape[0]
    tr = 512
    tile = pl.BlockSpec((tr, LANES), lambda i: (i, 0))
    return pl.pallas_call(
        functools.partial(_route_kernel, tr=tr),
        grid=(t_rows // tr,),
        in_specs=[tile],
        out_specs=[tile, tile, tile, pl.BlockSpec((1, LANES), lambda i: (0, 0))],
        out_shape=[jax.ShapeDtypeStruct((t_rows, LANES), jnp.int32), jax.ShapeDtypeStruct((t_rows, LANES), jnp.int32),
                   jax.ShapeDtypeStruct((t_rows, LANES), F32), jax.ShapeDtypeStruct((1, LANES), F32)],
        scratch_shapes=[pltpu.VMEM((1, LANES), F32)],
        compiler_params=_cparams(("arbitrary",)),
        name="route",
    )(logits)


MOE_BLK = 256


def _row_tile(ref2d, row):
    return ref2d.at[pl.ds(pl.multiple_of(row * ROW_SUB, ROW_SUB), ROW_SUB), :]


def _dispatch_kernel(dest_ref, cnt_ref, pad_ref, off_ref, h_ref, xs_ref, zero_ref, sem, zsem, *, td):
    i = pl.program_id(0)

    @pl.when(i == 0)
    def _():
        zero_ref[...] = jnp.zeros_like(zero_ref)
        for e in range(N_EXPERTS):
            base = off_ref[e]

            def fill(r, carry):
                pltpu.make_async_copy(zero_ref, _row_tile(xs_ref, base + r), zsem).start()
                return carry

            def drain(r, carry):
                pltpu.make_async_copy(zero_ref, _row_tile(xs_ref, base + r), zsem).wait()
                return carry

            lax.fori_loop(cnt_ref[e], pad_ref[e], fill, 0)
            lax.fori_loop(cnt_ref[e], pad_ref[e], drain, 0)

    def issue(r, carry):
        src = _row_tile(h_ref, r)
        for j in range(TOP_K):
            pltpu.make_async_copy(src, _row_tile(xs_ref, dest_ref[r * TOP_K + j]), sem).start(priority=j % 2)
        return carry

    lax.fori_loop(0, td, issue, 0)

    def drain_all(r, carry):
        src = _row_tile(h_ref, r)
        for j in range(TOP_K):
            pltpu.make_async_copy(src, _row_tile(xs_ref, dest_ref[r * TOP_K + j]), sem).wait()
        return carry

    lax.fori_loop(0, td, drain_all, 0)


def _dispatch_call(h2p, dest_flat, counts, padded, offsets, n_rows_sorted):
    t_rows = h2p.shape[0] // ROW_SUB
    td = 256
    smem = pl.BlockSpec(memory_space=pltpu.SMEM)
    return pl.pallas_call(
        functools.partial(_dispatch_kernel, td=td),
        grid=(t_rows // td,),
        in_specs=[pl.BlockSpec((td * TOP_K,), lambda i: (i,), memory_space=pltpu.SMEM), smem, smem, smem,
                  pl.BlockSpec((td * ROW_SUB, LANES), lambda i: (i, 0))],
        out_specs=pl.BlockSpec(memory_space=pl.ANY),
        out_shape=jax.ShapeDtypeStruct((n_rows_sorted * ROW_SUB, LANES), jnp.uint32),
        scratch_shapes=[pltpu.VMEM((ROW_SUB, LANES), jnp.uint32), pltpu.SemaphoreType.DMA(()),
                        pltpu.SemaphoreType.DMA(())],
        compiler_params=_cparams(("arbitrary",)),
        name="moe_dispatch",
    )(dest_flat, counts, padded, offsets, h2p)


def _expert_kernel(be_ref, na_ref, x_ref, w1_ref, b1_ref, w2_ref, b2_ref, y_ref, w1s_ref, w2s_ref):
    b = pl.program_id(0)
    prev = be_ref[jnp.maximum(b - 1, 0)]
    fresh = (b == 0) | (be_ref[b] != prev)

    @pl.when(fresh & (b < na_ref[0]))
    def _():
        for s in range(ROW_SUB):
            w1s_ref[s, :LANES, :] = w1_ref[s * LANES:(s + 1) * LANES, :].astype(BF16)
            w1s_ref[s, LANES:, :] = w1_ref[ROW_WORDS + s * LANES:ROW_WORDS + (s + 1) * LANES, :].astype(BF16)
        w2s_ref[...] = w2_ref[...].astype(BF16)

    @pl.when(b < na_ref[0])
    def _():
        acc = None
        for s in range(ROW_SUB):
            lo, hi = _unpack_words(x_ref[pl.ds(s, MOE_BLK, stride=ROW_SUB), :])
            xc = jnp.concatenate([lo.astype(BF16), hi.astype(BF16)], axis=1)
            part = jnp.dot(xc, w1s_ref[s], preferred_element_type=F32)
            acc = part if acc is None else acc + part
        a = acc + b1_ref[...]
        a_glu = jnp.minimum(a[:, :EXPERT_FF], SWIGLU_LIMIT)
        a_lin = jnp.clip(a[:, EXPERT_FF:], -SWIGLU_LIMIT, SWIGLU_LIMIT)
        act = a_glu * jax.nn.sigmoid(SWIGLU_ALPHA * a_glu) * (a_lin + 1.0)
        y = jnp.dot(act.astype(BF16), w2s_ref[...], preferred_element_type=F32) + b2_ref[...]
        _pack_rows(y, y_ref, MOE_BLK)


def _expert_call(xs, block_e, n_active, w1, b1, w2, b2, layer):
    n_blocks = block_e.shape[0]
    depth, n_exp, d, ff2 = w1.shape
    ff = w2.shape[2]

    def rows(b, be, na):
        return (jnp.minimum(b, na[0] - 1), 0)

    def wspec(r, c):
        return pl.BlockSpec((None, None, r, c), lambda b, be, na: (layer, be[b], 0, 0))

    return pl.pallas_call(
        _expert_kernel,
        grid_spec=pltpu.PrefetchScalarGridSpec(
            num_scalar_prefetch=2,
            grid=(n_blocks,),
            in_specs=[pl.BlockSpec((MOE_BLK * ROW_SUB, LANES), rows),
                      wspec(d, ff2), wspec(1, ff2), wspec(ff, d), wspec(1, d)],
            out_specs=pl.BlockSpec((MOE_BLK * ROW_SUB, LANES), rows),
            scratch_shapes=[pltpu.VMEM((ROW_SUB, 2 * LANES, ff2), BF16), pltpu.VMEM((ff, d), BF16)],
        ),
        out_shape=jax.ShapeDtypeStruct(xs.shape, jnp.uint32),
        compiler_params=_cparams(("arbitrary",), 56),
        name="moe_experts",
    )(block_e, n_active, xs, w1, b1.reshape(depth, n_exp, 1, ff2), w2, b2.reshape(depth, n_exp, 1, d))


def _combine_kernel(dest_ref, dest_next_ref, gate_ref, x_ref, g2_ref, fw_ref, ys_ref, o_ref, buf0_ref, buf1_ref,
                    sems, *, tc, final_norm):
    i = pl.program_id(0)
    n = pl.num_programs(0)
    bufs = (buf0_ref, buf1_ref)

    def row_copies(d_ref, b, r):
        return [pltpu.make_async_copy(_row_tile(ys_ref, d_ref[r * TOP_K + j]), _row_tile(bufs[b], j * tc + r),
                                      sems.at[b]) for j in range(TOP_K)]

    def start_tile(d_ref, b):
        def body(r, carry):
            for j, cp in enumerate(row_copies(d_ref, b, r)):
                cp.start(priority=j % 2)
            return carry

        lax.fori_loop(0, tc, body, 0)

    def wait_tile(b):
        def body(r, carry):
            for cp in row_copies(dest_ref, b, r):
                cp.wait()
            return carry

        lax.fori_loop(0, tc, body, 0)

    def reduce_tile(buf_ref):
        gate = gate_ref[...]
        gs = [gate[:, j:j + 1] for j in range(TOP_K)]
        lo_parts, hi_parts = [], []
        for s in range(ROW_SUB):
            f_lo = None
            f_hi = None
            for j in range(TOP_K):
                lo, hi = _unpack_words(buf_ref[pl.ds(j * tc * ROW_SUB + s, tc, stride=ROW_SUB), :])
                f_lo = gs[j] * lo if f_lo is None else f_lo + gs[j] * lo
                f_hi = gs[j] * hi if f_hi is None else f_hi + gs[j] * hi
            lo_parts.append(f_lo)
            hi_parts.append(f_hi)
        f = jnp.concatenate(lo_parts + hi_parts, axis=1)
        xn = x_ref[...] + g2_ref[...] * f
        if final_norm:
            ms = jnp.mean(xn * xn, axis=-1, keepdims=True)
            xn = xn * lax.rsqrt(ms + NORM_EPS) * fw_ref[...]
        o_ref[...] = xn

    @pl.when(i == 0)
    def _():
        start_tile(dest_ref, 0)

    for b in range(2):
        @pl.when((i % 2 == b) & (i + 1 < n))
        def _():
            start_tile(dest_next_ref, 1 - b)

    for b in range(2):
        @pl.when(i % 2 == b)
        def _():
            wait_tile(b)
            reduce_tile(bufs[b])


def _combine_call(xf, ys, dest_flat, gates, mods_l, final_w, *, n_batch, n_seq, final_norm):
    t_rows, d = xf.shape
    tc = 256
    tpb = n_seq // tc
    nlt = n_batch * tpb
    nt = t_rows // tc
    buf = pltpu.VMEM((TOP_K * tc * ROW_SUB, LANES), jnp.uint32)
    return pl.pallas_call(
        functools.partial(_combine_kernel, tc=tc, final_norm=final_norm),
        grid=(nt,),
        in_specs=[pl.BlockSpec((tc * TOP_K,), lambda i: (i,), memory_space=pltpu.SMEM),
                  pl.BlockSpec((tc * TOP_K,), lambda i: (jnp.minimum(i + 1, nt - 1),), memory_space=pltpu.SMEM),
                  pl.BlockSpec((tc, LANES), lambda i: (i, 0)),
                  pl.BlockSpec((tc, d), lambda i: (i, 0)),
                  pl.BlockSpec((None, None, 1, d), lambda i: (_mod_row(i, tpb, nlt, n_batch), 5, 0, 0)),
                  pl.BlockSpec((1, d), lambda i: (0, 0)),
                  pl.BlockSpec(memory_space=pl.ANY)],
        out_specs=pl.BlockSpec((tc, d), lambda i: (i, 0)),
        out_shape=jax.ShapeDtypeStruct((t_rows, d), F32),
        scratch_shapes=[buf, buf, pltpu.SemaphoreType.DMA((2,))],
        compiler_params=_cparams(("arbitrary",)),
        name="moe_combine",
    )(dest_flat, dest_flat, gates, xf, mods_l, final_w.reshape(1, d), ys)


def _moe_layer(xf, h2p, logits, mods_l, w1, b1, w2, b2, layer, final_w, *, n_batch, n_seq, final_norm):
    t_rows = xf.shape[0]
    idx, rank, gates, cnt = _route_call(logits)
    counts = cnt[0, :N_EXPERTS].astype(jnp.int32)
    padded = (counts + MOE_BLK - 1) // MOE_BLK * MOE_BLK
    ends = jnp.cumsum(padded)
    offsets = ends - padded
    dest = (offsets[idx[:, :TOP_K]] + rank[:, :TOP_K]).reshape(-1)
    n_blocks = (t_rows * TOP_K + N_EXPERTS * (MOE_BLK - 1)) // MOE_BLK
    block_row0 = jnp.arange(n_blocks, dtype=jnp.int32) * MOE_BLK
    block_e = jnp.minimum(jnp.sum((ends[None, :] <= block_row0[:, None]).astype(jnp.int32), axis=1), N_EXPERTS - 1)
    n_active = (ends[-1:] // MOE_BLK).astype(jnp.int32)
    xs = _dispatch_call(h2p, dest, counts, padded, offsets, n_blocks * MOE_BLK)
    ys = _expert_call(xs, block_e, n_active, w1, b1, w2, b2, layer)
    return _combine_call(xf, ys, dest, gates, mods_l, final_w, n_batch=n_batch, n_seq=n_seq, final_norm=final_norm)


W_IN_PIECES = ((0, 4096, COL_HY_X0), (5376, 7424, COL_GLA_V), (4352, 5376, COL_GLA_Q), (4096, 4352, COL_ATT_K),
               (7424, 7456, COL_DECAY))
W_IN_GATES_SRC = 7456
W_IN_COLS = 13600


def _w_prep_kernel(w_ref, o_ref):
    for lo, hi, dst in W_IN_PIECES:
        o_ref[:, dst:dst + hi - lo] = w_ref[:, lo:hi].astype(BF16)
    shift = W_IN_GATES_SRC % LANES
    base = W_IN_GATES_SRC - shift
    width = N_BRANCH * D_MODEL
    step = 512
    for c in range(0, width, step):
        end = min(base + c + step + LANES, W_IN_COLS)
        win = w_ref[:, base + c:end]
        o_ref[:, COL_GATES + c:COL_GATES + c + step] = win[:, shift:shift + step].astype(BF16)
    tail = COL_DECAY + GLA_RANK * 2
    o_ref[:, tail:] = jnp.zeros((o_ref.shape[0], P_PAD - tail), BF16)


def _permute_w_in(w_in, layer):
    d = w_in.shape[1]
    tr = 128
    return pl.pallas_call(
        _w_prep_kernel,
        grid=(d // tr,),
        in_specs=[pl.BlockSpec((None, tr, W_IN_COLS), lambda i: (layer, i, 0))],
        out_specs=pl.BlockSpec((tr, P_PAD), lambda i: (i, 0)),
        out_shape=jax.ShapeDtypeStruct((d, P_PAD), BF16),
        compiler_params=_cparams(("arbitrary",)),
        name="w_prep",
    )(w_in)


def kernel(x, c, ctx, c_ctx, ada_w, ada_b, norm1_w, norm2_w, w_in, hy_conv_w, hy_conv_b, hy_filt_w1, hy_filt_b1,
           hy_filt_freq, hy_filt_w2, hy_filt_b2, hy_filt_w3, hy_decay, hy_bias, attn_sink, gla_a_w, gla_a_b,
           gla_norm_w, w_branch, w_out, router_w, router_b, moe_w1, moe_b1, moe_w2, moe_b2, final_norm_w):
    n_batch, n_seq, d = x.shape
    n_ctx = ctx.shape[1]
    depth = ada_w.shape[0]
    t_lat = n_batch * n_seq
    xf = jnp.concatenate([x.reshape(t_lat, d), ctx.reshape(n_batch * n_ctx, d)], axis=0)
    cond = jnp.zeros((SUBLANES, d), F32).at[:n_batch].set(c).at[n_batch].set(c_ctx)
    mods = _ada_call(cond, ada_w, ada_b).reshape(depth, SUBLANES, N_ADA, 1, d)
    rw_pad = jnp.pad(router_w, ((0, 0), (0, 0), (0, LANES - N_EXPERTS)))
    rb_pad = jnp.pad(router_b, ((0, 0), (0, LANES - N_EXPERTS))).reshape(depth, 1, LANES)
    cos_t, sin_t = _rope_tables(n_seq)
    for l in range(depth):
        need_ctx = l < depth - 1
        last = l == depth - 1
        mods_l = mods[l]
        p = _in_call(xf, norm1_w[l], mods_l, _permute_w_in(w_in, l), n_batch=n_batch, n_seq=n_seq)
        hy_params = (hy_conv_w[l], hy_conv_b[l], hy_filt_w1[l], hy_filt_b1[l], hy_filt_freq[l], hy_filt_w2[l],
                     hy_filt_b2[l], hy_filt_w3[l], hy_decay[l], hy_bias[l])
        y_a = _hyena_layer(p, hy_params, n_batch=n_batch, seq_len=n_seq, row0=0)
        kvp = _kv_prep_call(p, cos_t, sin_t, n_batch=n_batch, n_seq=n_seq)
        y_b = _attn_call(attn_sink[l], p, kvp, cos_t, sin_t, n_batch=n_batch, n_seq=n_seq, n_ctx=n_ctx,
                         q_row0=0, ctx_row0=t_lat, has_local=True)
        y_c, y_cc = _gla_layer(p, gla_a_w[l], gla_a_b[l], gla_norm_w[l], n_batch=n_batch, n_seq=n_seq,
                               n_ctx=n_ctx, need_ctx=need_ctx)
        ys_ctx = None
        if need_ctx:
            y_ac = _hyena_layer(p, hy_params, n_batch=n_batch, seq_len=n_ctx, row0=t_lat)
            y_bc = _attn_call(attn_sink[l], p, kvp, None, None, n_batch=n_batch, n_seq=n_ctx, n_ctx=n_ctx,
                              q_row0=t_lat, ctx_row0=t_lat, has_local=False)
            ys_ctx = (y_ac, y_bc, y_cc)
        n_rows = xf.shape[0] if need_ctx else t_lat
        xf, h2p, logits = _merge_call(xf, p, mods_l, norm2_w[l], (y_a, y_b, y_c), ys_ctx,
                                      w_branch[l].astype(BF16), w_out[l].astype(BF16),
                                      rw_pad[l], rb_pad[l], n_batch=n_batch, n_seq=n_seq, n_rows=n_rows)
        xf = _moe_layer(xf, h2p, logits, mods_l, moe_w1, moe_b1, moe_w2, moe_b2, l, final_norm_w,
                        n_batch=n_batch, n_seq=n_seq, final_norm=last)
    return xf.reshape(n_batch, n_seq, d)
```

```python
import functools
import math

import jax
import jax.numpy as jnp
import numpy as np
from jax import lax
from jax.experimental import pallas as pl
from jax.experimental.pallas import tpu as pltpu

BF16 = jnp.bfloat16
F32 = jnp.float32

D_MODEL = 2048
DEPTH = 4
GRID_W = 64
NORM_EPS = 1e-5
N_ADA = 6
MIX_WIDTH = D_MODEL // 2

HY_WIDTH = MIX_WIDTH
HY_BANDS = 16
HY_EMB = 1 + 2 * HY_BANDS
HY_FFN = 64

ATT_HEAD_DIM = 64
ATT_HEADS = MIX_WIDTH // ATT_HEAD_DIM
ATT_KV_HEADS = 2
ATT_GROUP = ATT_HEADS // ATT_KV_HEADS
WINDOW = 128
ATT_BLOCK = 128
ROPE_BASE = 10000.0
NEG_INF = -1e30

GLA_HEADS = 4
GLA_DV = MIX_WIDTH // GLA_HEADS
GLA_DK = GLA_DV // 2
GLA_RANK = 16
GLA_TAU = 16.0
GLA_CHUNK = 64

N_BRANCH = 3
N_EXPERTS = 32
TOP_K = 4
EXPERT_FF = D_MODEL // 4
SWIGLU_LIMIT = 7.0
SWIGLU_ALPHA = 1.702

LANES = 128
SUBLANES = 8

COL_HY_X0 = 0
COL_HY_X1 = 1024
COL_HY_V = 2048
COL_ATT_Q = 3072
COL_GLA_V = 4096
COL_GLA_R = 5120
COL_GATES = 6144
COL_GLA_Q = 12288
COL_GLA_K = 12800
COL_ATT_K = 13312
COL_ATT_V = 13440
COL_DECAY = 13568
P_PAD = 13824


def _cparams(dims, vmem_mb=48):
    return pltpu.CompilerParams(dimension_semantics=dims, vmem_limit_bytes=vmem_mb << 20)


def _split_bf16(a):
    hi = a.astype(BF16)
    lo = (a - hi.astype(F32)).astype(BF16)
    return hi, lo


def _ada_kernel(c_ref, w_ref, b_ref, o_ref):
    cv = c_ref[...]
    s_hi, s_lo = _split_bf16(cv * jax.nn.sigmoid(cv))
    w_hi, w_lo = _split_bf16(w_ref[...])
    acc = jnp.dot(s_hi, w_hi, preferred_element_type=F32)
    acc += jnp.dot(s_lo, w_hi, preferred_element_type=F32)
    acc += jnp.dot(s_hi, w_lo, preferred_element_type=F32)
    o_ref[...] = acc + b_ref[...]


def _ada_call(cond8, ada_w, ada_b):
    depth, d, n6 = ada_w.shape
    tn = 1024
    return pl.pallas_call(
        _ada_kernel,
        grid=(depth, n6 // tn),
        in_specs=[
            pl.BlockSpec((SUBLANES, d), lambda l, j: (0, 0)),
            pl.BlockSpec((None, d, tn), lambda l, j: (l, 0, j)),
            pl.BlockSpec((None, 1, tn), lambda l, j: (l, 0, j)),
        ],
        out_specs=pl.BlockSpec((None, SUBLANES, tn), lambda l, j: (l, 0, j)),
        out_shape=jax.ShapeDtypeStruct((depth, SUBLANES, n6), F32),
        compiler_params=_cparams(("arbitrary", "arbitrary")),
        name="ada",
    )(cond8, ada_w, ada_b.reshape(depth, 1, n6))


def _rms_modulate(xf, nw, sh, sc1):
    ms = jnp.mean(xf * xf, axis=-1, keepdims=True)
    return (xf * lax.rsqrt(ms + NORM_EPS) * nw) * sc1 + sh


def _in_kernel(x_ref, nw_ref, sh_ref, sc_ref, w_ref, o_ref, h_ref, *, tm, rc):
    @pl.when(pl.program_id(1) == 0)
    def _():
        nw = nw_ref[...]
        sh = sh_ref[...]
        sc1 = 1.0 + sc_ref[...]

        def body(r, carry):
            sl = pl.ds(pl.multiple_of(r * rc, rc), rc)
            h_ref[sl, :] = _rms_modulate(x_ref[sl, :], nw, sh, sc1).astype(BF16)
            return carry

        lax.fori_loop(0, tm // rc, body, 0)

    o_ref[...] = jnp.dot(h_ref[...], w_ref[...], preferred_element_type=F32).astype(o_ref.dtype)


def _token_tile(n_lat_per_batch, n_ctx_total, cap):
    tm = cap
    while n_lat_per_batch % tm or n_ctx_total % tm:
        tm //= 2
    return tm


def _mod_row(i, tiles_per_batch, n_lat_tiles, n_batch):
    return jnp.where(i < n_lat_tiles, i // tiles_per_batch, n_batch)


def _in_call(xf, norm_w, mods_l, w_bf, *, n_batch, n_seq):
    t_rows, d = xf.shape
    p_pad = w_bf.shape[1]
    tm = _token_tile(n_seq, t_rows - n_batch * n_seq, 1024)
    tn = 1536
    tpb = n_seq // tm
    nlt = n_batch * tpb

    def mod_spec(comp):
        return pl.BlockSpec((None, None, 1, d), lambda i, j: (_mod_row(i, tpb, nlt, n_batch), comp, 0, 0))

    return pl.pallas_call(
        functools.partial(_in_kernel, tm=tm, rc=min(tm, 64)),
        grid=(t_rows // tm, p_pad // tn),
        in_specs=[
            pl.BlockSpec((tm, d), lambda i, j: (i, 0)),
            pl.BlockSpec((1, d), lambda i, j: (0, 0)),
            mod_spec(0),
            mod_spec(1),
            pl.BlockSpec((d, tn), lambda i, j: (0, j)),
        ],
        out_specs=pl.BlockSpec((tm, tn), lambda i, j: (i, j)),
        out_shape=jax.ShapeDtypeStruct((t_rows, p_pad), BF16),
        scratch_shapes=[pltpu.VMEM((tm, d), BF16)],
        compiler_params=_cparams(("arbitrary", "arbitrary")),
        name="in_proj",
    )(xf, norm_w.reshape(1, d), mods_l, mods_l, w_bf)


def _rope_tables(n_tokens):
    pos = np.arange(n_tokens)
    row_pos = (pos // GRID_W).astype(np.float32)
    col_pos = (pos % GRID_W).astype(np.float32)
    n_freq = ATT_HEAD_DIM // 4
    inv = (ROPE_BASE ** (-np.arange(n_freq, dtype=np.float32) / n_freq)).astype(np.float32)
    lane = np.arange(LANES)
    dd = lane % ATT_HEAD_DIM
    axis = dd // (2 * n_freq)
    freq = dd % n_freq
    second = (dd % (2 * n_freq)) >= n_freq
    p = jnp.where(jnp.asarray(axis)[None, :] == 0, jnp.asarray(row_pos)[:, None], jnp.asarray(col_pos)[:, None])
    ang = p * jnp.asarray(inv)[jnp.asarray(freq)][None, :]
    sign = jnp.where(jnp.asarray(second), 1.0, -1.0).astype(F32)
    return jnp.cos(ang), jnp.sin(ang) * sign[None, :]


def _rope_tile(xf, cos, sin_signed):
    lane = lax.broadcasted_iota(jnp.int32, xf.shape, 1)
    first = (lane % 32) < 16
    partner = jnp.where(first, pltpu.roll(xf, LANES - 16, 1), pltpu.roll(xf, 16, 1))
    return xf * cos + partner * sin_signed


KV_SLOTS = 4


def _kv_prep_kernel(k_ref, v_ref, cos_ref, sin_ref, o_ref, *, n_lat_tiles):
    kf = k_ref[...].astype(F32)
    kf = jnp.where(pl.program_id(0) < n_lat_tiles, _rope_tile(kf, cos_ref[...], sin_ref[...]), kf)
    vf = v_ref[...].astype(F32)
    half = LANES // 2
    o_ref[:, 0:LANES] = kf.astype(BF16)
    o_ref[:, LANES:2 * LANES] = pltpu.roll(kf, half, 1).astype(BF16)
    o_ref[:, 2 * LANES:3 * LANES] = vf.astype(BF16)
    o_ref[:, 3 * LANES:4 * LANES] = pltpu.roll(vf, half, 1).astype(BF16)


def _kv_prep_call(p, cos_t, sin_t, *, n_batch, n_seq):
    t_rows = p.shape[0]
    tr = _token_tile(n_seq, t_rows - n_batch * n_seq, 256)
    nb = n_seq // tr
    nlt = n_batch * nb
    tab = pl.BlockSpec((tr, LANES), lambda i: (jnp.where(i < nlt, i % nb, 0), 0))
    return pl.pallas_call(
        functools.partial(_kv_prep_kernel, n_lat_tiles=nlt),
        grid=(t_rows // tr,),
        in_specs=[pl.BlockSpec((tr, LANES), lambda i: (i, COL_ATT_K // LANES)),
                  pl.BlockSpec((tr, LANES), lambda i: (i, COL_ATT_V // LANES)), tab, tab],
        out_specs=pl.BlockSpec((tr, KV_SLOTS * LANES), lambda i: (i, 0)),
        out_shape=jax.ShapeDtypeStruct((t_rows, KV_SLOTS * LANES), BF16),
        compiler_params=_cparams(("arbitrary",)),
        name="kv_prep",
    )(p, p, cos_t, sin_t)


def _attn_kernel(*refs, has_local, n_seq):
    if has_local:
        sink_ref, q_ref, kvp_ref, kvc_ref, kvn_ref, kvx_ref, cos_ref, sin_ref, o_ref = refs
    else:
        sink_ref, q_ref, kvx_ref, o_ref = refs
    i = pl.program_id(1)
    npair = ATT_GROUP // 2
    rows = npair * ATT_BLOCK
    n_loc = 3 * ATT_BLOCK if has_local else 0
    if has_local:
        kv_rows = jnp.concatenate([kvp_ref[...], kvc_ref[...], kvn_ref[...], kvx_ref[...]], axis=0)
        r_i = lax.broadcasted_iota(jnp.int32, (rows, n_loc), 0)
        c_i = lax.broadcasted_iota(jnp.int32, (rows, n_loc), 1)
        qpos = i * ATT_BLOCK + (r_i & (ATT_BLOCK - 1))
        kpos = (i - 1) * ATT_BLOCK + c_i
        allowed = (kpos >= 0) & (kpos < n_seq) & (jnp.abs(qpos - kpos) <= WINDOW)
        cos = cos_ref[...]
        sin = sin_ref[...]
    else:
        kv_rows = kvx_ref[...]
    n_keys = kv_rows.shape[0]
    low = lax.broadcasted_iota(jnp.int32, (n_keys, LANES), 1) < ATT_HEAD_DIM
    out_low = lax.broadcasted_iota(jnp.int32, (rows, LANES), 1) < ATT_HEAD_DIM
    pair_of_row = lax.broadcasted_iota(jnp.int32, (rows, 1), 0) // ATT_BLOCK
    zero = jnp.zeros((n_keys, LANES), BF16)
    nt = (((1,), (1,)), ((), ()))

    def slot(s):
        return kv_rows[:, s * LANES:(s + 1) * LANES]

    for g in range(ATT_KV_HEADS):
        ka, kb, va, vb = (slot(0), slot(1), slot(2), slot(3)) if g == 0 else (slot(1), slot(0), slot(3), slot(2))
        k2 = jnp.concatenate([jnp.where(low, ka, zero), jnp.where(low, zero, kb)], axis=0)
        v2 = jnp.concatenate([jnp.where(low, va, zero), jnp.where(low, zero, vb)], axis=0)
        qs = []
        for pr in range(npair):
            hp = g * npair + pr
            qf = q_ref[:, hp * LANES:(hp + 1) * LANES].astype(F32)
            if has_local:
                qf = _rope_tile(qf, cos, sin)
            qs.append((qf * (ATT_HEAD_DIM ** -0.5)).astype(BF16))
        s = lax.dot_general(jnp.concatenate(qs, axis=0), k2, nt, preferred_element_type=F32)
        probs, dens = [], []
        for hf in range(2):
            sink = jnp.zeros((rows, 1), F32)
            for pr in range(npair):
                sink = jnp.where(pair_of_row == pr, sink_ref[(g * npair + pr) * 2 + hf], sink)
            sh = s[:, hf * n_keys:(hf + 1) * n_keys]
            s_x = sh[:, n_loc:]
            m = jnp.maximum(jnp.max(s_x, axis=-1, keepdims=True), sink)
            if has_local:
                s_l = jnp.where(allowed, sh[:, :n_loc], NEG_INF)
                m = jnp.maximum(m, jnp.max(s_l, axis=-1, keepdims=True))
                e_l = jnp.exp(s_l - m)
                probs.append(e_l.astype(BF16))
            e_x = jnp.exp(s_x - m)
            probs.append(e_x.astype(BF16))
            den = jnp.sum(e_x, axis=-1, keepdims=True) + jnp.exp(sink - m)
            if has_local:
                den = den + jnp.sum(e_l, axis=-1, keepdims=True)
            dens.append(den)
        o = jnp.dot(jnp.concatenate(probs, axis=1), v2, preferred_element_type=F32)
        o = o * jnp.where(out_low, 1.0 / dens[0], 1.0 / dens[1])
        for pr in range(npair):
            hp = g * npair + pr
            o_ref[:, hp * LANES:(hp + 1) * LANES] = o[pr * ATT_BLOCK:(pr + 1) * ATT_BLOCK].astype(o_ref.dtype)


def _attn_call(sink, p, kvp, cos_t, sin_t, *, n_batch, n_seq, n_ctx, q_row0, ctx_row0, has_local):
    nq = n_seq // ATT_BLOCK
    qb0 = q_row0 // ATT_BLOCK
    cb0 = ctx_row0 // n_ctx
    kvw = KV_SLOTS * LANES

    def loc(d):
        return pl.BlockSpec((ATT_BLOCK, kvw), lambda b, i: (qb0 + b * nq + jnp.clip(i + d, 0, nq - 1), 0))

    in_specs = [pl.BlockSpec(memory_space=pltpu.SMEM),
                pl.BlockSpec((ATT_BLOCK, MIX_WIDTH), lambda b, i: (qb0 + b * nq + i, COL_ATT_Q // MIX_WIDTH))]
    args = [sink, p]
    if has_local:
        in_specs += [loc(-1), loc(0), loc(1)]
        args += [kvp, kvp, kvp]
    in_specs += [pl.BlockSpec((n_ctx, kvw), lambda b, i: (cb0 + b, 0))]
    args += [kvp]
    if has_local:
        in_specs += [pl.BlockSpec((ATT_BLOCK, LANES), lambda b, i: (i, 0)),
                     pl.BlockSpec((ATT_BLOCK, LANES), lambda b, i: (i, 0))]
        args += [cos_t, sin_t]
    return pl.pallas_call(
        functools.partial(_attn_kernel, has_local=has_local, n_seq=n_seq),
        grid=(n_batch, nq),
        in_specs=in_specs,
        out_specs=pl.BlockSpec((ATT_BLOCK, MIX_WIDTH), lambda b, i: (b * nq + i, 0)),
        out_shape=jax.ShapeDtypeStruct((n_batch * n_seq, MIX_WIDTH), BF16),
        compiler_params=_cparams(("arbitrary", "arbitrary")),
        name="attn_local" if has_local else "attn_ctx",
    )(*args)


def _gla_kernel(*refs, reverse, fuse_out, tb):
    if fuse_out:
        (q_ref, k_ref, v_ref, pa_ref, aw_ref, ab_ref, s0_ref, oo_ref, r_ref, nw_ref,
         y_ref, sout_ref, st_ref, la_ref) = refs
    else:
        q_ref, k_ref, v_ref, pa_ref, aw_ref, ab_ref, s0_ref, y_ref, sout_ref, st_ref, la_ref = refs
    i = pl.program_id(1)
    ck = GLA_CHUNK
    nchunk = tb // ck

    @pl.when(i == 0)
    def _():
        st_ref[...] = s0_ref[...]

    z = jnp.dot(pa_ref[...], aw_ref[...], preferred_element_type=F32) + ab_ref[...]
    la_ref[...] = (jnp.minimum(z, 0.0) - jnp.log(1.0 + jnp.exp(-jnp.abs(z)))) * (1.0 / GLA_TAU)

    r_i = lax.broadcasted_iota(jnp.int32, (ck, ck), 0)
    c_i = lax.broadcasted_iota(jnp.int32, (ck, ck), 1)
    tri = (c_i >= r_i) if reverse else (c_i <= r_i)
    tri_b = jnp.where(tri, 1.0, 0.0).astype(BF16)
    nt = (((1,), (1,)), ((), ()))
    tn = (((0,), (0,)), ((), ()))
    if fuse_out:
        nw = nw_ref[...]

    def chunk(ci):
        c = (nchunk - 1 - ci) if reverse else ci
        rs = pl.ds(pl.multiple_of(c * ck, ck), ck)
        la_hi, la_lo = _split_bf16(la_ref[rs, :])
        b = jnp.dot(tri_b, la_hi, preferred_element_type=F32) + jnp.dot(tri_b, la_lo, preferred_element_type=F32)
        b_tot = b[0:1, :] if reverse else b[ck - 1:ck, :]
        kf = k_ref[rs, :].astype(F32)
        q_in = (q_ref[rs, :].astype(F32) * (GLA_DK ** -0.5) * jnp.exp(b)).astype(BF16)
        k_in = (kf * jnp.exp(-b)).astype(BF16)
        k_st = (kf * jnp.exp(b_tot - b)).astype(BF16)
        dec = jnp.exp(b_tot)
        vv = v_ref[rs, :]
        for h in range(GLA_HEADS):
            hs = slice(h * GLA_DK, (h + 1) * GLA_DK)
            vs = slice(h * GLA_DV, (h + 1) * GLA_DV)
            att = lax.dot_general(q_in[:, hs], k_in[:, hs], nt, preferred_element_type=F32)
            att = jnp.where(tri, att, 0.0).astype(BF16)
            st = st_ref[h]
            o = jnp.dot(att, vv[:, vs], preferred_element_type=F32)
            o = o + lax.dot_general(q_in[:, hs], st.astype(BF16), nt, preferred_element_type=F32)
            st_ref[h] = st * dec[:, hs] + lax.dot_general(vv[:, vs], k_st[:, hs], tn, preferred_element_type=F32)
            if fuse_out:
                tot = o + oo_ref[rs, vs]
                yn = tot * lax.rsqrt(jnp.mean(tot * tot, axis=-1, keepdims=True) + NORM_EPS) * nw
                rf = r_ref[rs, vs].astype(F32)
                y_ref[rs, vs] = (yn * (rf * jax.nn.sigmoid(rf))).astype(y_ref.dtype)
            else:
                y_ref[rs, vs] = o

    per_iter = 2 if nchunk % 2 == 0 else 1

    def body(it, carry):
        for u in range(per_iter):
            chunk(it * per_iter + u)
        return carry

    lax.fori_loop(0, nchunk // per_iter, body, 0)

    @pl.when(i == pl.num_programs(1) - 1)
    def _():
        sout_ref[...] = st_ref[...]


def _gla_call(p, aw_pad, ab, s0, o_other, norm_w, *, n_batch, n_seq, row0, reverse):
    fuse_out = o_other is not None
    tb = min(n_seq, 512)
    nblk = n_seq // tb
    rb0 = row0 // tb

    def blk(i):
        return (nblk - 1 - i) if reverse else i

    def prow(width, col):
        return pl.BlockSpec((tb, width), lambda b, i: (rb0 + b * nblk + blk(i), col // width))

    def orow(width):
        return pl.BlockSpec((tb, width), lambda b, i: (b * nblk + blk(i), 0))

    st_spec = pl.BlockSpec((None, GLA_HEADS, GLA_DV, GLA_DK), lambda b, i: (b, 0, 0, 0))
    nq = GLA_HEADS * GLA_DK
    in_specs = [prow(nq, COL_GLA_Q), prow(nq, COL_GLA_K), prow(MIX_WIDTH, COL_GLA_V), prow(LANES, COL_DECAY),
                pl.BlockSpec((LANES, nq), lambda b, i: (0, 0)), pl.BlockSpec((1, nq), lambda b, i: (0, 0)), st_spec]
    args = [p, p, p, p, aw_pad, ab, s0]
    if fuse_out:
        in_specs += [orow(MIX_WIDTH), prow(MIX_WIDTH, COL_GLA_R), pl.BlockSpec((1, GLA_DV), lambda b, i: (0, 0))]
        args += [o_other, p, norm_w.reshape(1, GLA_DV)]
    return pl.pallas_call(
        functools.partial(_gla_kernel, reverse=reverse, fuse_out=fuse_out, tb=tb),
        grid=(n_batch, nblk),
        in_specs=in_specs,
        out_specs=[orow(MIX_WIDTH), st_spec],
        out_shape=[jax.ShapeDtypeStruct((n_batch * n_seq, MIX_WIDTH), BF16 if fuse_out else F32),
                   jax.ShapeDtypeStruct((n_batch, GLA_HEADS, GLA_DV, GLA_DK), F32)],
        scratch_shapes=[pltpu.VMEM((GLA_HEADS, GLA_DV, GLA_DK), F32), pltpu.VMEM((tb, nq), F32)],
        compiler_params=_cparams(("arbitrary", "arbitrary")),
        name="gla_" + ("bwd" if reverse else "fwd") + ("_out" if fuse_out else ""),
    )(*args)


def _gla_decay_weights(a_w, a_b):
    nq = GLA_HEADS * GLA_DK
    pads = []
    for d in range(2):
        z = jnp.zeros((LANES, nq), F32).at[d * GLA_RANK:(d + 1) * GLA_RANK].set(a_w[d])
        pads.append(z.astype(BF16))
    return pads, [a_b[0].reshape(1, nq), a_b[1].reshape(1, nq)]


def _gla_layer(p, a_w, a_b, norm_w, *, n_batch, n_seq, n_ctx, need_ctx):
    (aw_f, aw_b), (ab_f, ab_b) = _gla_decay_weights(a_w, a_b)
    t_lat = n_batch * n_seq
    zero = jnp.zeros((n_batch, GLA_HEADS, GLA_DV, GLA_DK), F32)
    kw_c = dict(n_batch=n_batch, n_seq=n_ctx, row0=t_lat)
    kw_l = dict(n_batch=n_batch, n_seq=n_seq, row0=0)
    oc_b, s_b = _gla_call(p, aw_b, ab_b, zero, None, None, reverse=True, **kw_c)
    if need_ctx:
        y_cc, s_f = _gla_call(p, aw_f, ab_f, zero, oc_b, norm_w, reverse=False, **kw_c)
    else:
        y_cc = None
        _, s_f = _gla_call(p, aw_f, ab_f, zero, None, None, reverse=False, **kw_c)
    ol_b, _ = _gla_call(p, aw_b, ab_b, s_b, None, None, reverse=True, **kw_l)
    y_c, _ = _gla_call(p, aw_f, ab_f, s_f, ol_b, norm_w, reverse=False, **kw_l)
    return y_c, y_cc


HY_N2 = 128
HY_J = SUBLANES
HY_G = HY_N2 // HY_J
HY_CB = 128
HY_TILE = 2 * HY_J


def _dot3(a, b):
    a_hi, a_lo = _split_bf16(a)
    b_hi, b_lo = _split_bf16(b)
    acc = jnp.dot(a_hi, b_hi, preferred_element_type=F32)
    acc += jnp.dot(a_lo, b_hi, preferred_element_type=F32)
    acc += jnp.dot(a_hi, b_lo, preferred_element_type=F32)
    return acc


def _hy_filter_kernel(z_ref, w1_ref, b1_ref, fr_ref, w2_ref, b2_ref, w3_ref, dec_ref, o_ref):
    z = z_ref[...]
    fr = fr_ref[...]
    h = jnp.sin(fr * (_dot3(z, w1_ref[...]) + b1_ref[...]))
    h = jnp.sin(fr * (_dot3(h, w2_ref[...]) + b2_ref[...]))
    t = z[:, 0:1]
    o_ref[...] = _dot3(h, w3_ref[...]) * jnp.exp(-t * jnp.abs(dec_ref[...]))


def _hy_filter_call(seq_len, w1, b1, freq, w2, b2, w3, decay):
    pos = np.arange(seq_len, dtype=np.float32)
    t = pos / np.float32(max(seq_len - 1, 1))
    ang = (np.float32(2.0 * math.pi) * pos / np.float32(seq_len)).astype(np.float32)
    bands = np.linspace(1e-4, HY_BANDS - 1, HY_BANDS, dtype=np.float32)
    fw = (ang[:, None] * bands[None, :]).astype(np.float32)
    z = jnp.concatenate([jnp.asarray(t)[:, None], jnp.cos(jnp.asarray(fw)), -jnp.sin(jnp.asarray(fw))], axis=-1)
    z = jnp.pad(z, ((0, 0), (0, LANES - HY_EMB)))
    padh = LANES - HY_FFN
    w1p = jnp.pad(w1, ((0, LANES - HY_EMB), (0, padh)))
    w2p = jnp.pad(w2, ((0, padh), (0, padh)))
    w3p = jnp.pad(w3, ((0, padh), (0, 0)))
    row = lambda v: jnp.pad(v, (0, padh)).reshape(1, LANES)
    n_out = w3.shape[1]
    tl = min(seq_len, 256)
    full = lambda shape: pl.BlockSpec(shape, lambda i: (0, 0))
    return pl.pallas_call(
        _hy_filter_kernel,
        grid=(seq_len // tl,),
        in_specs=[pl.BlockSpec((tl, LANES), lambda i: (i, 0)), full((LANES, LANES)), full((1, LANES)),
                  full((1, LANES)), full((LANES, LANES)), full((1, LANES)), full((LANES, n_out)), full((1, n_out))],
        out_specs=pl.BlockSpec((tl, n_out), lambda i: (i, 0)),
        out_shape=jax.ShapeDtypeStruct((seq_len, n_out), F32),
        compiler_params=_cparams(("arbitrary",)),
        name="hy_filter",
    )(z, w1p, row(b1), row(freq), w2p, row(b2), w3p, decay.reshape(1, n_out))


@functools.lru_cache(maxsize=None)
def _hy_constants(seq_len):
    nfft = 2 * seq_len
    n2, jj, g = HY_N2, HY_J, HY_G
    n1 = nfft // n2
    h1 = n1 // 2
    k1 = np.arange(n1)

    def kron_mat(rows_n, cols_n, sign, both_parts_in):
        m = np.zeros((rows_n, 2, jj, cols_n, 2 if both_parts_in else 1, jj))
        for j in range(jj):
            r = np.arange(rows_n)[:, None]
            c = np.arange(cols_n)[None, :]
            kk = r if sign < 0 else c
            ph = sign * 2.0 * np.pi * (r * c / n1 + kk * j / nfft)
            cr, ci = np.cos(ph), np.sin(ph)
            m[:, 0, j, :, 0, j] = cr
            m[:, 1, j, :, 0, j] = ci
            if both_parts_in:
                m[:, 0, j, :, 1, j] = -ci
                m[:, 1, j, :, 1, j] = cr
        return m.reshape(rows_n * 2 * jj, -1)

    ma = kron_mat(n1, h1, -1.0, True)
    mak = kron_mat(n1, h1, -1.0, False)
    mai = kron_mat(h1, n1, +1.0, True)

    nn = (np.arange(g)[:, None] * jj + np.arange(jj)[None, :])
    kk2 = np.arange(n2)
    ph = -2.0 * np.pi * kk2[:, None, None] * nn[None] / n2
    mc = np.zeros((2, n2, g, 2, jj))
    mc[0, :, :, 0, :] = np.cos(ph)
    mc[0, :, :, 1, :] = -np.sin(ph)
    mc[1, :, :, 0, :] = np.sin(ph)
    mc[1, :, :, 1, :] = np.cos(ph)
    mc = mc.reshape(2 * n2, 2 * n2)
    mci = np.zeros((g, 2, jj, 2, n2))
    phi = np.transpose(-ph, (1, 2, 0))
    mci[:, 0, :, 0, :] = np.cos(phi)
    mci[:, 0, :, 1, :] = -np.sin(phi)
    mci[:, 1, :, 0, :] = np.sin(phi)
    mci[:, 1, :, 1, :] = np.cos(phi)
    mci = mci.reshape(2 * n2, 2 * n2)

    th = 2.0 * np.pi * (np.arange(g)[:, None] * jj) * k1[None, :] / nfft
    twc = np.cos(th).astype(np.float32).reshape(-1)
    tws = (-np.sin(th)).astype(np.float32).reshape(-1)
    to_bf = lambda m: m.astype(np.float32).astype(BF16)
    return dict(ma=to_bf(ma), mak=to_bf(mak), mai=to_bf(mai), mc=to_bf(mc), mci=to_bf(mci), twc=twc, tws=tws,
                n1=n1, h1=h1)


def _hy_twiddle_tiles(y, g, twc_ref, tws_ref, n1, conj):
    tiles = []
    for k1 in range(n1):
        c = twc_ref[g * n1 + k1]
        s = tws_ref[g * n1 + k1]
        if conj:
            s = -s
        yr = y[HY_TILE * k1:HY_TILE * k1 + HY_J]
        yi = y[HY_TILE * k1 + HY_J:HY_TILE * (k1 + 1)]
        tiles.append(jnp.concatenate([c * yr - s * yi, s * yr + c * yi], axis=0).astype(BF16))
    return tiles


def _hy_tile_rows(k1, g):
    return pl.ds(pl.multiple_of(k1 * 2 * HY_N2 + g * HY_TILE, HY_TILE), HY_TILE)


def _hy_khat_kernel(twc_ref, tws_ref, hf_ref, hb_ref, mak_ref, mc_ref, kh_ref, a_ref, *, n1):
    nfft = n1 * HY_N2
    h1 = n1 // 2
    cb = hf_ref.shape[1]

    def stage_a(g, carry):
        def slab(ref, r):
            return ref[pl.ds(pl.multiple_of(r * HY_N2 + g * HY_J, HY_J), HY_J), :]

        fs = [slab(hf_ref, r) for r in range(h1)]
        bs = [slab(hb_ref, r) for r in range(h1)]
        first = (lax.broadcasted_iota(jnp.int32, (HY_J, cb), 0) == 0) & (g == 0)
        bs[0] = jnp.where(first, 0.0, bs[0])
        rhs = jnp.concatenate([jnp.concatenate(fs, axis=0), jnp.concatenate(bs, axis=0)], axis=1).astype(BF16)
        y = jnp.dot(mak_ref[...], rhs, preferred_element_type=F32)
        for k1, t in enumerate(_hy_twiddle_tiles(y, g, twc_ref, tws_ref, n1, False)):
            a_ref[_hy_tile_rows(k1, g), :] = t
        return carry

    lax.fori_loop(0, HY_G, stage_a, 0)

    kc = min(4, n1)

    def stage_c(kq, carry):
        blks = [pl.ds(pl.multiple_of((kq * kc + u) * 2 * HY_N2, 2 * HY_N2), 2 * HY_N2) for u in range(kc)]
        x = jnp.dot(mc_ref[...], jnp.concatenate([a_ref[b, :] for b in blks], axis=1),
                    preferred_element_type=F32) * (1.0 / nfft)
        for u, b in enumerate(blks):
            xf, xb = x[:, 2 * u * cb:(2 * u + 1) * cb], x[:, (2 * u + 1) * cb:(2 * u + 2) * cb]
            kh_ref[b, :] = jnp.concatenate([xf[:HY_N2] + xb[:HY_N2], xf[HY_N2:] - xb[HY_N2:]], axis=0).astype(BF16)
        return carry

    lax.fori_loop(0, n1 // kc, stage_c, 0)


def _hy_khat_call(h, consts):
    seq_len, c2 = h.shape
    c = c2 // 2
    n1 = consts["n1"]
    rows = n1 * 2 * HY_N2
    ncb = c // HY_CB
    return pl.pallas_call(
        functools.partial(_hy_khat_kernel, n1=n1),
        grid=(ncb,),
        in_specs=[pl.BlockSpec(memory_space=pltpu.SMEM), pl.BlockSpec(memory_space=pltpu.SMEM),
                  pl.BlockSpec((seq_len, HY_CB), lambda i: (0, i)),
                  pl.BlockSpec((seq_len, HY_CB), lambda i: (0, ncb + i)),
                  pl.BlockSpec(consts["mak"].shape, lambda i: (0, 0)),
                  pl.BlockSpec(consts["mc"].shape, lambda i: (0, 0))],
        out_specs=pl.BlockSpec((rows, HY_CB), lambda i: (0, i)),
        out_shape=jax.ShapeDtypeStruct((rows, c), BF16),
        scratch_shapes=[pltpu.VMEM((rows, 2 * HY_CB), BF16)],
        compiler_params=_cparams(("arbitrary",)),
        name="hy_khat",
    )(jnp.asarray(consts["twc"]), jnp.asarray(consts["tws"]), h, h, jnp.asarray(consts["mak"]),
      jnp.asarray(consts["mc"]))


def _hy_conv3_chunk(ref, base, w, b, r, n_chunks, seq_len):
    r0 = pl.multiple_of(r * HY_N2, HY_N2)
    cur = ref[pl.ds(base + r0, HY_N2), :].astype(F32)
    pr0 = pl.multiple_of(jnp.maximum(r0 - HY_TILE, 0), HY_TILE)
    nx0 = pl.multiple_of(jnp.minimum(r0 + HY_N2, seq_len - HY_TILE), HY_TILE)
    prev_row = ref[pl.ds(base + pr0, HY_TILE), :].astype(F32)[HY_TILE - 1:HY_TILE]
    next_row = ref[pl.ds(base + nx0, HY_TILE), :].astype(F32)[0:1]
    prev_row = jnp.where(r > 0, prev_row, 0.0)
    next_row = jnp.where(r < n_chunks - 1, next_row, 0.0)
    row = lax.broadcasted_iota(jnp.int32, cur.shape, 0)
    up = jnp.where(row == 0, prev_row, pltpu.roll(cur, 1, 0))
    dn = jnp.where(row == HY_N2 - 1, next_row, pltpu.roll(cur, HY_N2 - 1, 0))
    return up * w[0:1] + cur * w[1:2] + dn * w[2:3] + b


def _hy_interleave(a, b):
    parts = []
    for g in range(HY_G):
        parts.append(a[g * HY_J:(g + 1) * HY_J])
        parts.append(b[g * HY_J:(g + 1) * HY_J])
    return jnp.concatenate(parts, axis=0)


def _hy_conv_kernel(twc_ref, tws_ref, x0_ref, x1_ref, v_ref,
                    cw0_ref, cw1_ref, cw2_ref, cb0_ref, cb1_ref, cb2_ref, bias_ref, kh_ref,
                    ma_ref, mc_ref, mci_ref, mai_ref, o_ref, zc_ref, x0s_ref, a_ref, *, n1, seq_len):
    h1 = n1 // 2
    tile_rows = 2 * HY_N2

    def prologue(r, carry):
        cw0, cw1, cw2 = cw0_ref[...], cw1_ref[...], cw2_ref[...]
        cb0, cb1, cb2 = cb0_ref[...], cb1_ref[...], cb2_ref[...]
        zs, x0s = [], []
        for base in (0, seq_len):
            x0s.append(_hy_conv3_chunk(x0_ref, base, cw0, cb0, r, h1, seq_len))
            x1c = _hy_conv3_chunk(x1_ref, base, cw1, cb1, r, h1, seq_len)
            vc = _hy_conv3_chunk(v_ref, base, cw2, cb2, r, h1, seq_len)
            zs.append(x1c * vc)
        dst = pl.ds(pl.multiple_of(r * tile_rows, tile_rows), tile_rows)
        zc_ref[dst, :] = _hy_interleave(zs[0], zs[1]).astype(BF16)
        x0s_ref[dst, :] = _hy_interleave(x0s[0], x0s[1]).astype(BF16)
        return carry

    lax.fori_loop(0, h1, prologue, 0)

    cb = o_ref.shape[1]
    ga = 2
    kc = min(4, n1)

    def stage_a(gp, carry):
        groups = [gp * ga + u for u in range(ga)]
        rhs = jnp.concatenate(
            [jnp.concatenate([zc_ref[_hy_tile_rows(r, g), :] for r in range(h1)], axis=0) for g in groups], axis=1)
        y = jnp.dot(ma_ref[...], rhs, preferred_element_type=F32)
        for u, g in enumerate(groups):
            for k1, t in enumerate(_hy_twiddle_tiles(y[:, u * cb:(u + 1) * cb], g, twc_ref, tws_ref, n1, False)):
                a_ref[_hy_tile_rows(k1, g), :] = t
        return carry

    lax.fori_loop(0, HY_G // ga, stage_a, 0)

    def stage_c(kq, carry):
        blks = [pl.ds(pl.multiple_of((kq * kc + u) * tile_rows, tile_rows), tile_rows) for u in range(kc)]
        x = jnp.dot(mc_ref[...], jnp.concatenate([a_ref[b, :] for b in blks], axis=1), preferred_element_type=F32)
        pms = []
        for u, b in enumerate(blks):
            kh = kh_ref[b, :].astype(F32)
            xr, xi = x[:HY_N2, u * cb:(u + 1) * cb], x[HY_N2:, u * cb:(u + 1) * cb]
            kr, ki = kh[:HY_N2], kh[HY_N2:]
            pms.append(jnp.concatenate([xr * kr - xi * ki, xr * ki + xi * kr], axis=0).astype(BF16))
        z = jnp.dot(mci_ref[...], jnp.concatenate(pms, axis=1), preferred_element_type=F32).astype(BF16)
        for u, b in enumerate(blks):
            a_ref[b, :] = z[:, u * cb:(u + 1) * cb]
        return carry

    lax.fori_loop(0, n1 // kc, stage_c, 0)

    bias = bias_ref[...]

    def stage_ai(gp, carry):
        groups = [gp * ga + u for u in range(ga)]
        cols = []
        for g in groups:
            tiles = []
            for k1 in range(n1):
                t = a_ref[_hy_tile_rows(k1, g), :].astype(F32)
                c = twc_ref[g * n1 + k1]
                s = -tws_ref[g * n1 + k1]
                tr, ti = t[:HY_J], t[HY_J:]
                tiles.append(jnp.concatenate([c * tr - s * ti, s * tr + c * ti], axis=0).astype(BF16))
            cols.append(jnp.concatenate(tiles, axis=0))
        y = jnp.dot(mai_ref[...], jnp.concatenate(cols, axis=1), preferred_element_type=F32)
        for u, g in enumerate(groups):
            for r in range(h1):
                src = _hy_tile_rows(r, g)
                z = zc_ref[src, :].astype(F32)
                x0 = x0s_ref[src, :].astype(F32)
                o = x0 * (y[HY_TILE * r:HY_TILE * (r + 1), u * cb:(u + 1) * cb] + bias * z)
                dst = pl.ds(pl.multiple_of(r * HY_N2 + g * HY_J, HY_J), HY_J)
                o_ref[dst, :] = o[:HY_J]
                dst2 = pl.ds(pl.multiple_of(seq_len + r * HY_N2 + g * HY_J, HY_J), HY_J)
                o_ref[dst2, :] = o[HY_J:]
        return carry

    lax.fori_loop(0, HY_G // ga, stage_ai, 0)


def _hy_conv_call(p, khat, conv_w, conv_b, bias, consts, *, n_batch, seq_len, row0):
    c = HY_WIDTH
    n1 = consts["n1"]
    h1 = consts["h1"]
    rb0 = row0 // (2 * seq_len)
    ncb = c // HY_CB
    tile_rows = 2 * HY_N2

    def pin(col):
        return pl.BlockSpec((2 * seq_len, HY_CB), lambda cb, pi: (rb0 + pi, col // HY_CB + cb))

    def cws(stream):
        return pl.BlockSpec((3, HY_CB), lambda cb, pi: (0, stream * ncb + cb))

    def cbs(stream):
        return pl.BlockSpec((1, HY_CB), lambda cb, pi: (0, stream * ncb + cb))

    const = lambda a: pl.BlockSpec(a.shape, lambda cb, pi: (0, 0))
    mats = [jnp.asarray(consts[k]) for k in ("ma", "mc", "mci", "mai")]
    in_specs = [pl.BlockSpec(memory_space=pltpu.SMEM), pl.BlockSpec(memory_space=pltpu.SMEM),
                pin(COL_HY_X0), pin(COL_HY_X1), pin(COL_HY_V), cws(0), cws(1), cws(2), cbs(0), cbs(1), cbs(2),
                pl.BlockSpec((1, HY_CB), lambda cb, pi: (0, cb)),
                pl.BlockSpec((n1 * tile_rows, HY_CB), lambda cb, pi: (0, cb))] + [const(m) for m in mats]
    return pl.pallas_call(
        functools.partial(_hy_conv_kernel, n1=n1, seq_len=seq_len),
        grid=(ncb, n_batch // 2),
        in_specs=in_specs,
        out_specs=pl.BlockSpec((2 * seq_len, HY_CB), lambda cb, pi: (pi, cb)),
        out_shape=jax.ShapeDtypeStruct((n_batch * seq_len, c), F32),
        scratch_shapes=[pltpu.VMEM((h1 * tile_rows, HY_CB), BF16), pltpu.VMEM((h1 * tile_rows, HY_CB), BF16),
                        pltpu.VMEM((n1 * tile_rows, HY_CB), BF16)],
        compiler_params=_cparams(("arbitrary", "arbitrary"), 56),
        name="hy_conv",
    )(jnp.asarray(consts["twc"]), jnp.asarray(consts["tws"]), p, p, p,
      conv_w, conv_w, conv_w, conv_b.reshape(1, -1), conv_b.reshape(1, -1), conv_b.reshape(1, -1),
      bias.reshape(1, -1), khat, *mats)


def _hyena_layer(p, params, *, n_batch, seq_len, row0):
    conv_w, conv_b, w1, b1, freq, w2, b2, w3, decay, bias = params
    consts = _hy_constants(seq_len)
    h = _hy_filter_call(seq_len, w1, b1, freq, w2, b2, w3, decay)
    khat = _hy_khat_call(h, consts)
    return _hy_conv_call(p, khat, conv_w, conv_b, bias, consts, n_batch=n_batch, seq_len=seq_len, row0=row0)


ROW_WORDS = D_MODEL // 2
ROW_SUB = ROW_WORDS // LANES


def _pack_rows(val, ref2d, n_rows):
    lo = lax.bitcast_convert_type(val[:, :ROW_WORDS].astype(BF16).astype(F32), jnp.uint32)
    hi = lax.bitcast_convert_type(val[:, ROW_WORDS:].astype(BF16).astype(F32), jnp.uint32)
    word = (hi & jnp.uint32(0xFFFF0000)) | (lo >> 16)
    for s in range(ROW_SUB):
        ref2d[pl.ds(s, n_rows, stride=ROW_SUB), :] = word[:, s * LANES:(s + 1) * LANES]


def _unpack_words(w):
    lo = lax.bitcast_convert_type(w << 16, F32)
    hi = lax.bitcast_convert_type(w & jnp.uint32(0xFFFF0000), F32)
    return lo, hi


def _merge_kernel(*refs, has_ctx, n_lat_tiles, tm):
    if has_ctx:
        (x_ref, g1_ref, sh2_ref, sc2_ref, nw2_ref, gl0_ref, gl1_ref, gl2_ref, ya_l, yb_l, yc_l, ya_c, yb_c, yc_c,
         wb_ref, wo_ref, rw_ref, rb_ref, xo_ref, h2p_ref, lg_ref) = refs
    else:
        (x_ref, g1_ref, sh2_ref, sc2_ref, nw2_ref, gl0_ref, gl1_ref, gl2_ref, ya_l, yb_l, yc_l,
         wb_ref, wo_ref, rw_ref, rb_ref, xo_ref, h2p_ref, lg_ref) = refs
    ys = [ya_l[...].astype(BF16), yb_l[...], yc_l[...]]
    if has_ctx:
        is_ctx = pl.program_id(0) >= n_lat_tiles
        ys = [jnp.where(is_ctx, c, l) for c, l in zip((ya_c[...].astype(BF16), yb_c[...], yc_c[...]), ys)]
    m = None
    for g, gl_ref in enumerate((gl0_ref, gl1_ref, gl2_ref)):
        proj = jnp.dot(ys[g], wb_ref[g], preferred_element_type=F32)
        term = jax.nn.sigmoid(gl_ref[...].astype(F32)) * proj
        m = term if m is None else m + term
    out = jnp.dot(m.astype(BF16), wo_ref[...], preferred_element_type=F32)
    xn = x_ref[...] + g1_ref[...] * out
    xo_ref[...] = xn
    h2 = _rms_modulate(xn, nw2_ref[...], sh2_ref[...], 1.0 + sc2_ref[...])
    lg_ref[...] = _dot3(h2, rw_ref[...]) + rb_ref[...]
    _pack_rows(h2, h2p_ref, tm)


def _merge_call(xf, p, mods_l, norm2_w, ys_lat, ys_ctx, wb_bf, wo_bf, rw_pad, rb_pad, *, n_batch, n_seq, n_rows):
    d = D_MODEL
    has_ctx = ys_ctx is not None
    tm = 256
    tpb = n_seq // tm
    nlt = n_batch * tpb
    nt = n_rows // tm

    def mod_spec(comp):
        return pl.BlockSpec((None, None, 1, d), lambda i: (_mod_row(i, tpb, nlt, n_batch), comp, 0, 0))

    def gate_spec(g):
        return pl.BlockSpec((tm, d), lambda i: (i, COL_GATES // d + g))

    lat_spec = pl.BlockSpec((tm, MIX_WIDTH), lambda i: (jnp.minimum(i, nlt - 1), 0))
    ctx_spec = pl.BlockSpec((tm, MIX_WIDTH), lambda i: (jnp.maximum(i - nlt, 0), 0))
    once = pl.Buffered(1)
    in_specs = [pl.BlockSpec((tm, d), lambda i: (i, 0)), mod_spec(2), mod_spec(3), mod_spec(4),
                pl.BlockSpec((1, d), lambda i: (0, 0)), gate_spec(0), gate_spec(1), gate_spec(2),
                lat_spec, lat_spec, lat_spec]
    args = [xf, mods_l, mods_l, mods_l, norm2_w.reshape(1, d), p, p, p, *ys_lat]
    if has_ctx:
        in_specs += [ctx_spec, ctx_spec, ctx_spec]
        args += list(ys_ctx)
    in_specs += [pl.BlockSpec((N_BRANCH, MIX_WIDTH, d), lambda i: (0, 0, 0), pipeline_mode=once),
                 pl.BlockSpec((d, d), lambda i: (0, 0), pipeline_mode=once),
                 pl.BlockSpec((d, LANES), lambda i: (0, 0)), pl.BlockSpec((1, LANES), lambda i: (0, 0))]
    args += [wb_bf, wo_bf, rw_pad, rb_pad]
    return pl.pallas_call(
        functools.partial(_merge_kernel, has_ctx=has_ctx, n_lat_tiles=nlt, tm=tm),
        grid=(nt,),
        in_specs=in_specs,
        out_specs=[pl.BlockSpec((tm, d), lambda i: (i, 0)),
                   pl.BlockSpec((tm * ROW_SUB, LANES), lambda i: (i, 0)),
                   pl.BlockSpec((tm, LANES), lambda i: (i, 0))],
        out_shape=[jax.ShapeDtypeStruct((n_rows, d), F32),
                   jax.ShapeDtypeStruct((n_rows * ROW_SUB, LANES), jnp.uint32),
                   jax.ShapeDtypeStruct((n_rows, LANES), F32)],
        compiler_params=_cparams(("arbitrary",), 58),
        name="merge",
    )(*args)


def _route_kernel(lg_ref, idx_ref, rank_ref, gate_ref, cnt_ref, carry_ref, *, tr):
    i = pl.program_id(0)

    @pl.when(i == 0)
    def _():
        carry_ref[...] = jnp.zeros_like(carry_ref)

    lane = lax.broadcasted_iota(jnp.int32, (tr, LANES), 1)
    lg = jnp.where(lane < N_EXPERTS, lg_ref[...], -jnp.inf)
    tops, hots = [], []
    for _ in range(TOP_K):
        mx = jnp.max(lg, axis=-1, keepdims=True)
        sel = jnp.min(jnp.where(lg == mx, lane, LANES), axis=-1, keepdims=True)
        hot = lane == sel
        tops.append((mx, sel))
        hots.append(hot)
        lg = jnp.where(hot, -jnp.inf, lg)
    es = [jnp.exp(mx - tops[0][0]) for mx, _ in tops]
    den = es[0] + es[1] + es[2] + es[3]
    hot_f = [jnp.where(h, 1.0, 0.0) for h in hots]
    chosen = hot_f[0] + hot_f[1] + hot_f[2] + hot_f[3]
    r_i = lax.broadcasted_iota(jnp.int32, (tr, tr), 0)
    c_i = lax.broadcasted_iota(jnp.int32, (tr, tr), 1)
    below = jnp.where(c_i < r_i, 1.0, 0.0).astype(BF16)
    before = jnp.dot(below, chosen.astype(BF16), preferred_element_type=F32) + carry_ref[...]
    idx_o = jnp.zeros((tr, LANES), jnp.int32)
    rank_o = jnp.zeros((tr, LANES), jnp.int32)
    gate_o = jnp.zeros((tr, LANES), F32)
    for j in range(TOP_K):
        rank_j = jnp.sum(hot_f[j] * before, axis=-1, keepdims=True).astype(jnp.int32)
        idx_o = jnp.where(lane == j, tops[j][1], idx_o)
        rank_o = jnp.where(lane == j, rank_j, rank_o)
        gate_o = jnp.where(lane == j, es[j] / den, gate_o)
    idx_ref[...] = idx_o
    rank_ref[...] = rank_o
    gate_ref[...] = gate_o
    carry_ref[...] += jnp.sum(chosen, axis=0, keepdims=True)
    cnt_ref[...] = carry_ref[...]


def _route_call(logits):
    t_rows = logits.shape[0]
    tr = 512
    tile = pl.BlockSpec((tr, LANES), lambda i: (i, 0))
    return pl.pallas_call(
        functools.partial(_route_kernel, tr=tr),
        grid=(t_rows // tr,),
        in_specs=[tile],
        out_specs=[tile, tile, tile, pl.BlockSpec((1, LANES), lambda i: (0, 0))],
        out_shape=[jax.ShapeDtypeStruct((t_rows, LANES), jnp.int32), jax.ShapeDtypeStruct((t_rows, LANES), jnp.int32),
                   jax.ShapeDtypeStruct((t_rows, LANES), F32), jax.ShapeDtypeStruct((1, LANES), F32)],
        scratch_shapes=[pltpu.VMEM((1, LANES), F32)],
        compiler_params=_cparams(("arbitrary",)),
        name="route",
    )(logits)


MOE_BLK = 256


def _row_tile(ref2d, row):
    return ref2d.at[pl.ds(pl.multiple_of(row * ROW_SUB, ROW_SUB), ROW_SUB), :]


def _dispatch_kernel(dest_ref, cnt_ref, pad_ref, off_ref, h_ref, xs_ref, zero_ref, sem, zsem, *, td):
    i = pl.program_id(0)

    @pl.when(i == 0)
    def _():
        zero_ref[...] = jnp.zeros_like(zero_ref)
        for e in range(N_EXPERTS):
            base = off_ref[e]

            def fill(r, carry):
                pltpu.make_async_copy(zero_ref, _row_tile(xs_ref, base + r), zsem).start()
                return carry

            def drain(r, carry):
                pltpu.make_async_copy(zero_ref, _row_tile(xs_ref, base + r), zsem).wait()
                return carry

            lax.fori_loop(cnt_ref[e], pad_ref[e], fill, 0)
            lax.fori_loop(cnt_ref[e], pad_ref[e], drain, 0)

    def issue(r, carry):
        src = _row_tile(h_ref, r)
        for j in range(TOP_K):
            pltpu.make_async_copy(src, _row_tile(xs_ref, dest_ref[r * TOP_K + j]), sem).start(priority=j % 2)
        return carry

    lax.fori_loop(0, td, issue, 0)

    def drain_all(r, carry):
        src = _row_tile(h_ref, r)
        for j in range(TOP_K):
            pltpu.make_async_copy(src, _row_tile(xs_ref, dest_ref[r * TOP_K + j]), sem).wait()
        return carry

    lax.fori_loop(0, td, drain_all, 0)


def _dispatch_call(h2p, dest_flat, counts, padded, offsets, n_rows_sorted):
    t_rows = h2p.shape[0] // ROW_SUB
    td = 256
    smem = pl.BlockSpec(memory_space=pltpu.SMEM)
    return pl.pallas_call(
        functools.partial(_dispatch_kernel, td=td),
        grid=(t_rows // td,),
        in_specs=[pl.BlockSpec((td * TOP_K,), lambda i: (i,), memory_space=pltpu.SMEM), smem, smem, smem,
                  pl.BlockSpec((td * ROW_SUB, LANES), lambda i: (i, 0))],
        out_specs=pl.BlockSpec(memory_space=pl.ANY),
        out_shape=jax.ShapeDtypeStruct((n_rows_sorted * ROW_SUB, LANES), jnp.uint32),
        scratch_shapes=[pltpu.VMEM((ROW_SUB, LANES), jnp.uint32), pltpu.SemaphoreType.DMA(()),
                        pltpu.SemaphoreType.DMA(())],
        compiler_params=_cparams(("arbitrary",)),
        name="moe_dispatch",
    )(dest_flat, counts, padded, offsets, h2p)


def _expert_kernel(row0_ref, nblk_ref, x_hbm, w1_ref, b1_ref, w2_ref, b2_ref, y_hbm, xbuf_ref, ybuf_ref,
                   w1s_ref, w2s_ref, xsem, ysem):
    e = pl.program_id(0)
    nb = nblk_ref[e]
    row0 = row0_ref[e]
    blk_rows = MOE_BLK * ROW_SUB

    def hbm_rows(ref, k):
        return ref.at[pl.ds(pl.multiple_of((row0 + k * MOE_BLK) * ROW_SUB, blk_rows), blk_rows), :]

    def x_copy(k, slot):
        return pltpu.make_async_copy(hbm_rows(x_hbm, k), xbuf_ref.at[slot], xsem.at[slot])

    def y_copy(k, slot):
        return pltpu.make_async_copy(ybuf_ref.at[slot], hbm_rows(y_hbm, k), ysem.at[slot])

    def compute(slot):
        acc = None
        for s in range(ROW_SUB):
            lo, hi = _unpack_words(xbuf_ref[slot, pl.ds(s, MOE_BLK, stride=ROW_SUB), :])
            xc = jnp.concatenate([lo.astype(BF16), hi.astype(BF16)], axis=1)
            part = jnp.dot(xc, w1s_ref[s], preferred_element_type=F32)
            acc = part if acc is None else acc + part
        a = acc + b1_ref[...]
        a_glu = jnp.minimum(a[:, :EXPERT_FF], SWIGLU_LIMIT)
        a_lin = jnp.clip(a[:, EXPERT_FF:], -SWIGLU_LIMIT, SWIGLU_LIMIT)
        act = a_glu * jax.nn.sigmoid(SWIGLU_ALPHA * a_glu) * (a_lin + 1.0)
        y = jnp.dot(act.astype(BF16), w2s_ref[...], preferred_element_type=F32) + b2_ref[...]
        _pack_rows(y, ybuf_ref.at[slot], MOE_BLK)

    @pl.when(nb > 0)
    def _():
        x_copy(0, 0).start()
        for s in range(ROW_SUB):
            w1s_ref[s, :LANES, :] = w1_ref[s * LANES:(s + 1) * LANES, :].astype(BF16)
            w1s_ref[s, LANES:, :] = w1_ref[ROW_WORDS + s * LANES:ROW_WORDS + (s + 1) * LANES, :].astype(BF16)
        w2s_ref[...] = w2_ref[...].astype(BF16)

        def pair(m, carry):
            for slot in range(2):
                k = 2 * m + slot

                @pl.when(k < nb)
                def _():
                    @pl.when(k + 1 < nb)
                    def _():
                        x_copy(k + 1, 1 - slot).start()

                    x_copy(k, slot).wait()

                    @pl.when(k >= 2)
                    def _():
                        y_copy(k - 2, slot).wait()

                    compute(slot)
                    y_copy(k, slot).start()
            return carry

        lax.fori_loop(0, (nb + 1) // 2, pair, 0)
        for slot in range(2):
            last = jnp.where((nb - 1) % 2 == slot, nb - 1, nb - 2)

            @pl.when(last >= 0)
            def _():
                y_copy(last, slot).wait()


def _expert_call(xs, row0, nblk, w1, b1, w2, b2, layer):
    depth, n_exp, d, ff2 = w1.shape
    ff = w2.shape[2]
    blk_rows = MOE_BLK * ROW_SUB

    def wspec(r, c):
        return pl.BlockSpec((None, None, r, c), lambda e, r0, nb: (layer, e, 0, 0))

    buf = pltpu.VMEM((2, blk_rows, LANES), jnp.uint32)
    return pl.pallas_call(
        _expert_kernel,
        grid_spec=pltpu.PrefetchScalarGridSpec(
            num_scalar_prefetch=2,
            grid=(n_exp,),
            in_specs=[pl.BlockSpec(memory_space=pl.ANY),
                      wspec(d, ff2), wspec(1, ff2), wspec(ff, d), wspec(1, d)],
            out_specs=pl.BlockSpec(memory_space=pl.ANY),
            scratch_shapes=[buf, buf, pltpu.VMEM((ROW_SUB, 2 * LANES, ff2), BF16), pltpu.VMEM((ff, d), BF16),
                            pltpu.SemaphoreType.DMA((2,)), pltpu.SemaphoreType.DMA((2,))],
        ),
        out_shape=jax.ShapeDtypeStruct(xs.shape, jnp.uint32),
        compiler_params=_cparams(("arbitrary",), 56),
        name="moe_experts",
    )(row0, nblk, xs, w1, b1.reshape(depth, n_exp, 1, ff2), w2, b2.reshape(depth, n_exp, 1, d))


def _combine_kernel(dest_ref, dest_next_ref, gate_ref, x_ref, g2_ref, fw_ref, ys_ref, o_ref, buf0_ref, buf1_ref,
                    sems, *, tc, final_norm):
    i = pl.program_id(0)
    n = pl.num_programs(0)
    bufs = (buf0_ref, buf1_ref)

    def row_copies(d_ref, b, r):
        return [pltpu.make_async_copy(_row_tile(ys_ref, d_ref[r * TOP_K + j]), _row_tile(bufs[b], j * tc + r),
                                      sems.at[b]) for j in range(TOP_K)]

    def start_tile(d_ref, b):
        def body(r, carry):
            for j, cp in enumerate(row_copies(d_ref, b, r)):
                cp.start(priority=j % 2)
            return carry

        lax.fori_loop(0, tc, body, 0)

    def wait_tile(b):
        def body(r, carry):
            for cp in row_copies(dest_ref, b, r):
                cp.wait()
            return carry

        lax.fori_loop(0, tc, body, 0)

    def reduce_tile(buf_ref):
        gate = gate_ref[...]
        gs = [gate[:, j:j + 1] for j in range(TOP_K)]
        lo_parts, hi_parts = [], []
        for s in range(ROW_SUB):
            f_lo = None
            f_hi = None
            for j in range(TOP_K):
                lo, hi = _unpack_words(buf_ref[pl.ds(j * tc * ROW_SUB + s, tc, stride=ROW_SUB), :])
                f_lo = gs[j] * lo if f_lo is None else f_lo + gs[j] * lo
                f_hi = gs[j] * hi if f_hi is None else f_hi + gs[j] * hi
            lo_parts.append(f_lo)
            hi_parts.append(f_hi)
        f = jnp.concatenate(lo_parts + hi_parts, axis=1)
        xn = x_ref[...] + g2_ref[...] * f
        if final_norm:
            ms = jnp.mean(xn * xn, axis=-1, keepdims=True)
            xn = xn * lax.rsqrt(ms + NORM_EPS) * fw_ref[...]
        o_ref[...] = xn

    @pl.when(i == 0)
    def _():
        start_tile(dest_ref, 0)

    for b in range(2):
        @pl.when((i % 2 == b) & (i + 1 < n))
        def _():
            start_tile(dest_next_ref, 1 - b)

    for b in range(2):
        @pl.when(i % 2 == b)
        def _():
            wait_tile(b)
            reduce_tile(bufs[b])


def _combine_call(xf, ys, dest_flat, gates, mods_l, final_w, *, n_batch, n_seq, final_norm):
    t_rows, d = xf.shape
    tc = 256
    tpb = n_seq // tc
    nlt = n_batch * tpb
    nt = t_rows // tc
    buf = pltpu.VMEM((TOP_K * tc * ROW_SUB, LANES), jnp.uint32)
    return pl.pallas_call(
        functools.partial(_combine_kernel, tc=tc, final_norm=final_norm),
        grid=(nt,),
        in_specs=[pl.BlockSpec((tc * TOP_K,), lambda i: (i,), memory_space=pltpu.SMEM),
                  pl.BlockSpec((tc * TOP_K,), lambda i: (jnp.minimum(i + 1, nt - 1),), memory_space=pltpu.SMEM),
                  pl.BlockSpec((tc, LANES), lambda i: (i, 0)),
                  pl.BlockSpec((tc, d), lambda i: (i, 0)),
                  pl.BlockSpec((None, None, 1, d), lambda i: (_mod_row(i, tpb, nlt, n_batch), 5, 0, 0)),
                  pl.BlockSpec((1, d), lambda i: (0, 0)),
                  pl.BlockSpec(memory_space=pl.ANY)],
        out_specs=pl.BlockSpec((tc, d), lambda i: (i, 0)),
        out_shape=jax.ShapeDtypeStruct((t_rows, d), F32),
        scratch_shapes=[buf, buf, pltpu.SemaphoreType.DMA((2,))],
        compiler_params=_cparams(("arbitrary",)),
        name="moe_combine",
    )(dest_flat, dest_flat, gates, xf, mods_l, final_w.reshape(1, d), ys)


def _moe_layer(xf, h2p, logits, mods_l, w1, b1, w2, b2, layer, final_w, *, n_batch, n_seq, final_norm):
    t_rows = xf.shape[0]
    idx, rank, gates, cnt = _route_call(logits)
    counts = cnt[0, :N_EXPERTS].astype(jnp.int32)
    padded = (counts + MOE_BLK - 1) // MOE_BLK * MOE_BLK
    ends = jnp.cumsum(padded)
    offsets = ends - padded
    dest = (offsets[idx[:, :TOP_K]] + rank[:, :TOP_K]).reshape(-1)
    n_blocks = (t_rows * TOP_K + N_EXPERTS * (MOE_BLK - 1)) // MOE_BLK
    xs = _dispatch_call(h2p, dest, counts, padded, offsets, n_blocks * MOE_BLK)
    ys = _expert_call(xs, offsets, padded // MOE_BLK, w1, b1, w2, b2, layer)
    return _combine_call(xf, ys, dest, gates, mods_l, final_w, n_batch=n_batch, n_seq=n_seq, final_norm=final_norm)


W_IN_PIECES = ((0, 4096, COL_HY_X0), (5376, 7424, COL_GLA_V), (4352, 5376, COL_GLA_Q), (4096, 4352, COL_ATT_K),
               (7424, 7456, COL_DECAY))
W_IN_GATES_SRC = 7456
W_IN_COLS = 13600


def _w_prep_kernel(w_ref, o_ref):
    for lo, hi, dst in W_IN_PIECES:
        o_ref[:, dst:dst + hi - lo] = w_ref[:, lo:hi].astype(BF16)
    shift = W_IN_GATES_SRC % LANES
    base = W_IN_GATES_SRC - shift
    width = N_BRANCH * D_MODEL
    step = 512
    for c in range(0, width, step):
        end = min(base + c + step + LANES, W_IN_COLS)
        win = w_ref[:, base + c:end]
        o_ref[:, COL_GATES + c:COL_GATES + c + step] = win[:, shift:shift + step].astype(BF16)
    tail = COL_DECAY + GLA_RANK * 2
    o_ref[:, tail:] = jnp.zeros((o_ref.shape[0], P_PAD - tail), BF16)


def _permute_w_in(w_in, layer):
    d = w_in.shape[1]
    tr = 128
    return pl.pallas_call(
        _w_prep_kernel,
        grid=(d // tr,),
        in_specs=[pl.BlockSpec((None, tr, W_IN_COLS), lambda i: (layer, i, 0))],
        out_specs=pl.BlockSpec((tr, P_PAD), lambda i: (i, 0)),
        out_shape=jax.ShapeDtypeStruct((d, P_PAD), BF16),
        compiler_params=_cparams(("arbitrary",)),
        name="w_prep",
    )(w_in)


def kernel(x, c, ctx, c_ctx, ada_w, ada_b, norm1_w, norm2_w, w_in, hy_conv_w, hy_conv_b, hy_filt_w1, hy_filt_b1,
           hy_filt_freq, hy_filt_w2, hy_filt_b2, hy_filt_w3, hy_decay, hy_bias, attn_sink, gla_a_w, gla_a_b,
           gla_norm_w, w_branch, w_out, router_w, router_b, moe_w1, moe_b1, moe_w2, moe_b2, final_norm_w):
    n_batch, n_seq, d = x.shape
    n_ctx = ctx.shape[1]
    depth = ada_w.shape[0]
    t_lat = n_batch * n_seq
    xf = jnp.concatenate([x.reshape(t_lat, d), ctx.reshape(n_batch * n_ctx, d)], axis=0)
    cond = jnp.zeros((SUBLANES, d), F32).at[:n_batch].set(c).at[n_batch].set(c_ctx)
    mods = _ada_call(cond, ada_w, ada_b).reshape(depth, SUBLANES, N_ADA, 1, d)
    rw_pad = jnp.pad(router_w, ((0, 0), (0, 0), (0, LANES - N_EXPERTS)))
    rb_pad = jnp.pad(router_b, ((0, 0), (0, LANES - N_EXPERTS))).reshape(depth, 1, LANES)
    cos_t, sin_t = _rope_tables(n_seq)
    for l in range(depth):
        need_ctx = l < depth - 1
        last = l == depth - 1
        mods_l = mods[l]
        p = _in_call(xf, norm1_w[l], mods_l, _permute_w_in(w_in, l), n_batch=n_batch, n_seq=n_seq)
        hy_params = (hy_conv_w[l], hy_conv_b[l], hy_filt_w1[l], hy_filt_b1[l], hy_filt_freq[l], hy_filt_w2[l],
                     hy_filt_b2[l], hy_filt_w3[l], hy_decay[l], hy_bias[l])
        y_a = _hyena_layer(p, hy_params, n_batch=n_batch, seq_len=n_seq, row0=0)
        kvp = _kv_prep_call(p, cos_t, sin_t, n_batch=n_batch, n_seq=n_seq)
        y_b = _attn_call(attn_sink[l], p, kvp, cos_t, sin_t, n_batch=n_batch, n_seq=n_seq, n_ctx=n_ctx,
                         q_row0=0, ctx_row0=t_lat, has_local=True)
        y_c, y_cc = _gla_layer(p, gla_a_w[l], gla_a_b[l], gla_norm_w[l], n_batch=n_batch, n_seq=n_seq,
                               n_ctx=n_ctx, need_ctx=need_ctx)
        ys_ctx = None
        if need_ctx:
            y_ac = _hyena_layer(p, hy_params, n_batch=n_batch, seq_len=n_ctx, row0=t_lat)
            y_bc = _attn_call(attn_sink[l], p, kvp, None, None, n_batch=n_batch, n_seq=n_ctx, n_ctx=n_ctx,
                              q_row0=t_lat, ctx_row0=t_lat, has_local=False)
            ys_ctx = (y_ac, y_bc, y_cc)
        n_rows = xf.shape[0] if need_ctx else t_lat
        xf, h2p, logits = _merge_call(xf, p, mods_l, norm2_w[l], (y_a, y_b, y_c), ys_ctx,
                                      w_branch[l].astype(BF16), w_out[l].astype(BF16),
                                      rw_pad[l], rb_pad[l], n_batch=n_batch, n_seq=n_seq, n_rows=n_rows)
        xf = _moe_layer(xf, h2p, logits, mods_l, moe_w1, moe_b1, moe_w2, moe_b2, l, final_norm_w,
                        n_batch=n_batch, n_seq=n_seq, final_norm=last)
    return xf.reshape(n_batch, n_seq, d)
```

```python
import functools
import math

import jax
import jax.numpy as jnp
import numpy as np
from jax import lax
from jax.experimental import pallas as pl
from jax.experimental.pallas import tpu as pltpu

BF16 = jnp.bfloat16
F32 = jnp.float32

D_MODEL = 2048
DEPTH = 4
GRID_W = 64
NORM_EPS = 1e-5
N_ADA = 6
MIX_WIDTH = D_MODEL // 2

HY_WIDTH = MIX_WIDTH
HY_BANDS = 16
HY_EMB = 1 + 2 * HY_BANDS
HY_FFN = 64

ATT_HEAD_DIM = 64
ATT_HEADS = MIX_WIDTH // ATT_HEAD_DIM
ATT_KV_HEADS = 2
ATT_GROUP = ATT_HEADS // ATT_KV_HEADS
WINDOW = 128
ATT_BLOCK = 128
ROPE_BASE = 10000.0
NEG_INF = -1e30

GLA_HEADS = 4
GLA_DV = MIX_WIDTH // GLA_HEADS
GLA_DK = GLA_DV // 2
GLA_RANK = 16
GLA_TAU = 16.0
GLA_CHUNK = 64

N_BRANCH = 3
N_EXPERTS = 32
TOP_K = 4
EXPERT_FF = D_MODEL // 4
SWIGLU_LIMIT = 7.0
SWIGLU_ALPHA = 1.702

LANES = 128
SUBLANES = 8

COL_HY_X0 = 0
COL_HY_X1 = 1024
COL_HY_V = 2048
COL_ATT_Q = 3072
COL_GLA_V = 4096
COL_GLA_R = 5120
COL_GATES = 6144
COL_GLA_Q = 12288
COL_GLA_K = 12800
COL_ATT_K = 13312
COL_ATT_V = 13440
COL_DECAY = 13568
P_PAD = 13824


def _cparams(dims, vmem_mb=48):
    return pltpu.CompilerParams(dimension_semantics=dims, vmem_limit_bytes=vmem_mb << 20)


def _split_bf16(a):
    hi = a.astype(BF16)
    lo = (a - hi.astype(F32)).astype(BF16)
    return hi, lo


def _ada_kernel(c_ref, w_ref, b_ref, o_ref):
    cv = c_ref[...]
    s_hi, s_lo = _split_bf16(cv * jax.nn.sigmoid(cv))
    w_hi, w_lo = _split_bf16(w_ref[...])
    acc = jnp.dot(s_hi, w_hi, preferred_element_type=F32)
    acc += jnp.dot(s_lo, w_hi, preferred_element_type=F32)
    acc += jnp.dot(s_hi, w_lo, preferred_element_type=F32)
    o_ref[...] = acc + b_ref[...]


def _ada_call(cond8, ada_w, ada_b):
    depth, d, n6 = ada_w.shape
    tn = 1024
    return pl.pallas_call(
        _ada_kernel,
        grid=(depth, n6 // tn),
        in_specs=[
            pl.BlockSpec((SUBLANES, d), lambda l, j: (0, 0)),
            pl.BlockSpec((None, d, tn), lambda l, j: (l, 0, j)),
            pl.BlockSpec((None, 1, tn), lambda l, j: (l, 0, j)),
        ],
        out_specs=pl.BlockSpec((None, SUBLANES, tn), lambda l, j: (l, 0, j)),
        out_shape=jax.ShapeDtypeStruct((depth, SUBLANES, n6), F32),
        compiler_params=_cparams(("arbitrary", "arbitrary")),
        name="ada",
    )(cond8, ada_w, ada_b.reshape(depth, 1, n6))


def _rms_modulate(xf, nw, sh, sc1):
    ms = jnp.mean(xf * xf, axis=-1, keepdims=True)
    return (xf * lax.rsqrt(ms + NORM_EPS) * nw) * sc1 + sh


def _in_kernel(x_ref, nw_ref, sh_ref, sc_ref, w_ref, o_ref, h_ref, *, tm, rc):
    @pl.when(pl.program_id(1) == 0)
    def _():
        nw = nw_ref[...]
        sh = sh_ref[...]
        sc1 = 1.0 + sc_ref[...]

        def body(r, carry):
            sl = pl.ds(pl.multiple_of(r * rc, rc), rc)
            h_ref[sl, :] = _rms_modulate(x_ref[sl, :], nw, sh, sc1).astype(BF16)
            return carry

        lax.fori_loop(0, tm // rc, body, 0)

    o_ref[...] = jnp.dot(h_ref[...], w_ref[...], preferred_element_type=F32).astype(o_ref.dtype)


def _token_tile(n_lat_per_batch, n_ctx_total, cap):
    tm = cap
    while n_lat_per_batch % tm or n_ctx_total % tm:
        tm //= 2
    return tm


def _mod_row(i, tiles_per_batch, n_lat_tiles, n_batch):
    return jnp.where(i < n_lat_tiles, i // tiles_per_batch, n_batch)


def _in_call(xf, norm_w, mods_l, w_bf, *, n_batch, n_seq):
    t_rows, d = xf.shape
    p_pad = w_bf.shape[1]
    tm = _token_tile(n_seq, t_rows - n_batch * n_seq, 1024)
    tn = 1536
    tpb = n_seq // tm
    nlt = n_batch * tpb

    def mod_spec(comp):
        return pl.BlockSpec((None, None, 1, d), lambda i, j: (_mod_row(i, tpb, nlt, n_batch), comp, 0, 0))

    return pl.pallas_call(
        functools.partial(_in_kernel, tm=tm, rc=min(tm, 64)),
        grid=(t_rows // tm, p_pad // tn),
        in_specs=[
            pl.BlockSpec((tm, d), lambda i, j: (i, 0)),
            pl.BlockSpec((1, d), lambda i, j: (0, 0)),
            mod_spec(0),
            mod_spec(1),
            pl.BlockSpec((d, tn), lambda i, j: (0, j)),
        ],
        out_specs=pl.BlockSpec((tm, tn), lambda i, j: (i, j)),
        out_shape=jax.ShapeDtypeStruct((t_rows, p_pad), BF16),
        scratch_shapes=[pltpu.VMEM((tm, d), BF16)],
        compiler_params=_cparams(("arbitrary", "arbitrary")),
        name="in_proj",
    )(xf, norm_w.reshape(1, d), mods_l, mods_l, w_bf)


def _rope_tables(n_tokens):
    pos = np.arange(n_tokens)
    row_pos = (pos // GRID_W).astype(np.float32)
    col_pos = (pos % GRID_W).astype(np.float32)
    n_freq = ATT_HEAD_DIM // 4
    inv = (ROPE_BASE ** (-np.arange(n_freq, dtype=np.float32) / n_freq)).astype(np.float32)
    lane = np.arange(LANES)
    dd = lane % ATT_HEAD_DIM
    axis = dd // (2 * n_freq)
    freq = dd % n_freq
    second = (dd % (2 * n_freq)) >= n_freq
    p = jnp.where(jnp.asarray(axis)[None, :] == 0, jnp.asarray(row_pos)[:, None], jnp.asarray(col_pos)[:, None])
    ang = p * jnp.asarray(inv)[jnp.asarray(freq)][None, :]
    sign = jnp.where(jnp.asarray(second), 1.0, -1.0).astype(F32)
    return jnp.cos(ang), jnp.sin(ang) * sign[None, :]


def _rope_tile(xf, cos, sin_signed):
    lane = lax.broadcasted_iota(jnp.int32, xf.shape, 1)
    first = (lane % 32) < 16
    partner = jnp.where(first, pltpu.roll(xf, LANES - 16, 1), pltpu.roll(xf, 16, 1))
    return xf * cos + partner * sin_signed


KV_SLOTS = 4


def _kv_prep_kernel(k_ref, v_ref, cos_ref, sin_ref, o_ref, *, n_lat_tiles):
    kf = k_ref[...].astype(F32)
    kf = jnp.where(pl.program_id(0) < n_lat_tiles, _rope_tile(kf, cos_ref[...], sin_ref[...]), kf)
    vf = v_ref[...].astype(F32)
    half = LANES // 2
    o_ref[:, 0:LANES] = kf.astype(BF16)
    o_ref[:, LANES:2 * LANES] = pltpu.roll(kf, half, 1).astype(BF16)
    o_ref[:, 2 * LANES:3 * LANES] = vf.astype(BF16)
    o_ref[:, 3 * LANES:4 * LANES] = pltpu.roll(vf, half, 1).astype(BF16)


def _kv_prep_call(p, cos_t, sin_t, *, n_batch, n_seq):
    t_rows = p.shape[0]
    tr = _token_tile(n_seq, t_rows - n_batch * n_seq, 256)
    nb = n_seq // tr
    nlt = n_batch * nb
    tab = pl.BlockSpec((tr, LANES), lambda i: (jnp.where(i < nlt, i % nb, 0), 0))
    return pl.pallas_call(
        functools.partial(_kv_prep_kernel, n_lat_tiles=nlt),
        grid=(t_rows // tr,),
        in_specs=[pl.BlockSpec((tr, LANES), lambda i: (i, COL_ATT_K // LANES)),
                  pl.BlockSpec((tr, LANES), lambda i: (i, COL_ATT_V // LANES)), tab, tab],
        out_specs=pl.BlockSpec((tr, KV_SLOTS * LANES), lambda i: (i, 0)),
        out_shape=jax.ShapeDtypeStruct((t_rows, KV_SLOTS * LANES), BF16),
        compiler_params=_cparams(("arbitrary",)),
        name="kv_prep",
    )(p, p, cos_t, sin_t)


def _attn_kernel(*refs, has_local, n_seq):
    if has_local:
        sink_ref, q_ref, kvp_ref, kvc_ref, kvn_ref, kvx_ref, cos_ref, sin_ref, o_ref = refs
    else:
        sink_ref, q_ref, kvx_ref, o_ref = refs
    i = pl.program_id(1)
    npair = ATT_GROUP // 2
    rows = npair * ATT_BLOCK
    n_loc = 3 * ATT_BLOCK if has_local else 0
    if has_local:
        kv_rows = jnp.concatenate([kvp_ref[...], kvc_ref[...], kvn_ref[...], kvx_ref[...]], axis=0)
        r_i = lax.broadcasted_iota(jnp.int32, (rows, n_loc), 0)
        c_i = lax.broadcasted_iota(jnp.int32, (rows, n_loc), 1)
        qpos = i * ATT_BLOCK + (r_i & (ATT_BLOCK - 1))
        kpos = (i - 1) * ATT_BLOCK + c_i
        allowed = (kpos >= 0) & (kpos < n_seq) & (jnp.abs(qpos - kpos) <= WINDOW)
        cos = cos_ref[...]
        sin = sin_ref[...]
    else:
        kv_rows = kvx_ref[...]
    n_keys = kv_rows.shape[0]
    low = lax.broadcasted_iota(jnp.int32, (n_keys, LANES), 1) < ATT_HEAD_DIM
    out_low = lax.broadcasted_iota(jnp.int32, (rows, LANES), 1) < ATT_HEAD_DIM
    pair_of_row = lax.broadcasted_iota(jnp.int32, (rows, 1), 0) // ATT_BLOCK
    zero = jnp.zeros((n_keys, LANES), BF16)
    nt = (((1,), (1,)), ((), ()))

    def slot(s):
        return kv_rows[:, s * LANES:(s + 1) * LANES]

    for g in range(ATT_KV_HEADS):
        ka, kb, va, vb = (slot(0), slot(1), slot(2), slot(3)) if g == 0 else (slot(1), slot(0), slot(3), slot(2))
        k2 = jnp.concatenate([jnp.where(low, ka, zero), jnp.where(low, zero, kb)], axis=0)
        v2 = jnp.concatenate([jnp.where(low, va, zero), jnp.where(low, zero, vb)], axis=0)
        qs = []
        for pr in range(npair):
            hp = g * npair + pr
            qf = q_ref[:, hp * LANES:(hp + 1) * LANES].astype(F32)
            if has_local:
                qf = _rope_tile(qf, cos, sin)
            qs.append((qf * (ATT_HEAD_DIM ** -0.5)).astype(BF16))
        s = lax.dot_general(jnp.concatenate(qs, axis=0), k2, nt, preferred_element_type=F32)
        probs, dens = [], []
        for hf in range(2):
            sink = jnp.zeros((rows, 1), F32)
            for pr in range(npair):
                sink = jnp.where(pair_of_row == pr, sink_ref[(g * npair + pr) * 2 + hf], sink)
            sh = s[:, hf * n_keys:(hf + 1) * n_keys]
            s_x = sh[:, n_loc:]
            m = jnp.maximum(jnp.max(s_x, axis=-1, keepdims=True), sink)
            if has_local:
                s_l = jnp.where(allowed, sh[:, :n_loc], NEG_INF)
                m = jnp.maximum(m, jnp.max(s_l, axis=-1, keepdims=True))
                e_l = jnp.exp(s_l - m)
                probs.append(e_l.astype(BF16))
            e_x = jnp.exp(s_x - m)
            probs.append(e_x.astype(BF16))
            den = jnp.sum(e_x, axis=-1, keepdims=True) + jnp.exp(sink - m)
            if has_local:
                den = den + jnp.sum(e_l, axis=-1, keepdims=True)
            dens.append(den)
        o = jnp.dot(jnp.concatenate(probs, axis=1), v2, preferred_element_type=F32)
        o = o * jnp.where(out_low, 1.0 / dens[0], 1.0 / dens[1])
        for pr in range(npair):
            hp = g * npair + pr
            o_ref[:, hp * LANES:(hp + 1) * LANES] = o[pr * ATT_BLOCK:(pr + 1) * ATT_BLOCK].astype(o_ref.dtype)


def _attn_call(sink, p, kvp, cos_t, sin_t, *, n_batch, n_seq, n_ctx, q_row0, ctx_row0, has_local):
    nq = n_seq // ATT_BLOCK
    qb0 = q_row0 // ATT_BLOCK
    cb0 = ctx_row0 // n_ctx
    kvw = KV_SLOTS * LANES

    def loc(d):
        return pl.BlockSpec((ATT_BLOCK, kvw), lambda b, i: (qb0 + b * nq + jnp.clip(i + d, 0, nq - 1), 0))

    in_specs = [pl.BlockSpec(memory_space=pltpu.SMEM),
                pl.BlockSpec((ATT_BLOCK, MIX_WIDTH), lambda b, i: (qb0 + b * nq + i, COL_ATT_Q // MIX_WIDTH))]
    args = [sink, p]
    if has_local:
        in_specs += [loc(-1), loc(0), loc(1)]
        args += [kvp, kvp, kvp]
    in_specs += [pl.BlockSpec((n_ctx, kvw), lambda b, i: (cb0 + b, 0))]
    args += [kvp]
    if has_local:
        in_specs += [pl.BlockSpec((ATT_BLOCK, LANES), lambda b, i: (i, 0)),
                     pl.BlockSpec((ATT_BLOCK, LANES), lambda b, i: (i, 0))]
        args += [cos_t, sin_t]
    return pl.pallas_call(
        functools.partial(_attn_kernel, has_local=has_local, n_seq=n_seq),
        grid=(n_batch, nq),
        in_specs=in_specs,
        out_specs=pl.BlockSpec((ATT_BLOCK, MIX_WIDTH), lambda b, i: (b * nq + i, 0)),
        out_shape=jax.ShapeDtypeStruct((n_batch * n_seq, MIX_WIDTH), BF16),
        compiler_params=_cparams(("arbitrary", "arbitrary")),
        name="attn_local" if has_local else "attn_ctx",
    )(*args)


def _gla_kernel(*refs, reverse, fuse_out, tb):
    if fuse_out:
        (q_ref, k_ref, v_ref, pa_ref, aw_ref, ab_ref, s0_ref, oo_ref, r_ref, nw_ref,
         y_ref, sout_ref, st_ref, la_ref) = refs
    else:
        q_ref, k_ref, v_ref, pa_ref, aw_ref, ab_ref, s0_ref, y_ref, sout_ref, st_ref, la_ref = refs
    i = pl.program_id(1)
    ck = GLA_CHUNK
    nchunk = tb // ck

    @pl.when(i == 0)
    def _():
        st_ref[...] = s0_ref[...]

    z = jnp.dot(pa_ref[...], aw_ref[...], preferred_element_type=F32) + ab_ref[...]
    la_ref[...] = (jnp.minimum(z, 0.0) - jnp.log(1.0 + jnp.exp(-jnp.abs(z)))) * (1.0 / GLA_TAU)

    r_i = lax.broadcasted_iota(jnp.int32, (ck, ck), 0)
    c_i = lax.broadcasted_iota(jnp.int32, (ck, ck), 1)
    tri = (c_i >= r_i) if reverse else (c_i <= r_i)
    tri_b = jnp.where(tri, 1.0, 0.0).astype(BF16)
    nt = (((1,), (1,)), ((), ()))
    tn = (((0,), (0,)), ((), ()))
    if fuse_out:
        nw = nw_ref[...]

    def chunk(ci):
        c = (nchunk - 1 - ci) if reverse else ci
        rs = pl.ds(pl.multiple_of(c * ck, ck), ck)
        la_hi, la_lo = _split_bf16(la_ref[rs, :])
        b = jnp.dot(tri_b, la_hi, preferred_element_type=F32) + jnp.dot(tri_b, la_lo, preferred_element_type=F32)
        b_tot = b[0:1, :] if reverse else b[ck - 1:ck, :]
        kf = k_ref[rs, :].astype(F32)
        q_in = (q_ref[rs, :].astype(F32) * (GLA_DK ** -0.5) * jnp.exp(b)).astype(BF16)
        k_in = (kf * jnp.exp(-b)).astype(BF16)
        k_st = (kf * jnp.exp(b_tot - b)).astype(BF16)
        dec = jnp.exp(b_tot)
        vv = v_ref[rs, :]
        for h in range(GLA_HEADS):
            hs = slice(h * GLA_DK, (h + 1) * GLA_DK)
            vs = slice(h * GLA_DV, (h + 1) * GLA_DV)
            att = lax.dot_general(q_in[:, hs], k_in[:, hs], nt, preferred_element_type=F32)
            att = jnp.where(tri, att, 0.0).astype(BF16)
            st = st_ref[h]
            o = jnp.dot(att, vv[:, vs], preferred_element_type=F32)
            o = o + lax.dot_general(q_in[:, hs], st.astype(BF16), nt, preferred_element_type=F32)
            st_ref[h] = st * dec[:, hs] + lax.dot_general(vv[:, vs], k_st[:, hs], tn, preferred_element_type=F32)
            if fuse_out:
                tot = o + oo_ref[rs, vs]
                yn = tot * lax.rsqrt(jnp.mean(tot * tot, axis=-1, keepdims=True) + NORM_EPS) * nw
                rf = r_ref[rs, vs].astype(F32)
                y_ref[rs, vs] = (yn * (rf * jax.nn.sigmoid(rf))).astype(y_ref.dtype)
            else:
                y_ref[rs, vs] = o

    per_iter = 2 if nchunk % 2 == 0 else 1

    def body(it, carry):
        for u in range(per_iter):
            chunk(it * per_iter + u)
        return carry

    lax.fori_loop(0, nchunk // per_iter, body, 0)

    @pl.when(i == pl.num_programs(1) - 1)
    def _():
        sout_ref[...] = st_ref[...]


def _gla_call(p, aw_pad, ab, s0, o_other, norm_w, *, n_batch, n_seq, row0, reverse):
    fuse_out = o_other is not None
    tb = min(n_seq, 512)
    nblk = n_seq // tb
    rb0 = row0 // tb

    def blk(i):
        return (nblk - 1 - i) if reverse else i

    def prow(width, col):
        return pl.BlockSpec((tb, width), lambda b, i: (rb0 + b * nblk + blk(i), col // width))

    def orow(width):
        return pl.BlockSpec((tb, width), lambda b, i: (b * nblk + blk(i), 0))

    st_spec = pl.BlockSpec((None, GLA_HEADS, GLA_DV, GLA_DK), lambda b, i: (b, 0, 0, 0))
    nq = GLA_HEADS * GLA_DK
    in_specs = [prow(nq, COL_GLA_Q), prow(nq, COL_GLA_K), prow(MIX_WIDTH, COL_GLA_V), prow(LANES, COL_DECAY),
                pl.BlockSpec((LANES, nq), lambda b, i: (0, 0)), pl.BlockSpec((1, nq), lambda b, i: (0, 0)), st_spec]
    args = [p, p, p, p, aw_pad, ab, s0]
    if fuse_out:
        in_specs += [orow(MIX_WIDTH), prow(MIX_WIDTH, COL_GLA_R), pl.BlockSpec((1, GLA_DV), lambda b, i: (0, 0))]
        args += [o_other, p, norm_w.reshape(1, GLA_DV)]
    return pl.pallas_call(
        functools.partial(_gla_kernel, reverse=reverse, fuse_out=fuse_out, tb=tb),
        grid=(n_batch, nblk),
        in_specs=in_specs,
        out_specs=[orow(MIX_WIDTH), st_spec],
        out_shape=[jax.ShapeDtypeStruct((n_batch * n_seq, MIX_WIDTH), BF16 if fuse_out else F32),
                   jax.ShapeDtypeStruct((n_batch, GLA_HEADS, GLA_DV, GLA_DK), F32)],
        scratch_shapes=[pltpu.VMEM((GLA_HEADS, GLA_DV, GLA_DK), F32), pltpu.VMEM((tb, nq), F32)],
        compiler_params=_cparams(("arbitrary", "arbitrary")),
        name="gla_" + ("bwd" if reverse else "fwd") + ("_out" if fuse_out else ""),
    )(*args)


def _gla_decay_weights(a_w, a_b):
    nq = GLA_HEADS * GLA_DK
    pads = []
    for d in range(2):
        z = jnp.zeros((LANES, nq), F32).at[d * GLA_RANK:(d + 1) * GLA_RANK].set(a_w[d])
        pads.append(z.astype(BF16))
    return pads, [a_b[0].reshape(1, nq), a_b[1].reshape(1, nq)]


def _gla_layer(p, a_w, a_b, norm_w, *, n_batch, n_seq, n_ctx, need_ctx):
    (aw_f, aw_b), (ab_f, ab_b) = _gla_decay_weights(a_w, a_b)
    t_lat = n_batch * n_seq
    zero = jnp.zeros((n_batch, GLA_HEADS, GLA_DV, GLA_DK), F32)
    kw_c = dict(n_batch=n_batch, n_seq=n_ctx, row0=t_lat)
    kw_l = dict(n_batch=n_batch, n_seq=n_seq, row0=0)
    oc_b, s_b = _gla_call(p, aw_b, ab_b, zero, None, None, reverse=True, **kw_c)
    if need_ctx:
        y_cc, s_f = _gla_call(p, aw_f, ab_f, zero, oc_b, norm_w, reverse=False, **kw_c)
    else:
        y_cc = None
        _, s_f = _gla_call(p, aw_f, ab_f, zero, None, None, reverse=False, **kw_c)
    ol_b, _ = _gla_call(p, aw_b, ab_b, s_b, None, None, reverse=True, **kw_l)
    y_c, _ = _gla_call(p, aw_f, ab_f, s_f, ol_b, norm_w, reverse=False, **kw_l)
    return y_c, y_cc


HY_N2 = 128
HY_J = SUBLANES
HY_G = HY_N2 // HY_J
HY_CB = 128
HY_TILE = 2 * HY_J


def _dot3(a, b):
    a_hi, a_lo = _split_bf16(a)
    b_hi, b_lo = _split_bf16(b)
    acc = jnp.dot(a_hi, b_hi, preferred_element_type=F32)
    acc += jnp.dot(a_lo, b_hi, preferred_element_type=F32)
    acc += jnp.dot(a_hi, b_lo, preferred_element_type=F32)
    return acc


def _hy_filter_kernel(z_ref, w1_ref, b1_ref, fr_ref, w2_ref, b2_ref, w3_ref, dec_ref, o_ref):
    z = z_ref[...]
    fr = fr_ref[...]
    h = jnp.sin(fr * (_dot3(z, w1_ref[...]) + b1_ref[...]))
    h = jnp.sin(fr * (_dot3(h, w2_ref[...]) + b2_ref[...]))
    t = z[:, 0:1]
    o_ref[...] = _dot3(h, w3_ref[...]) * jnp.exp(-t * jnp.abs(dec_ref[...]))


def _hy_filter_call(seq_len, w1, b1, freq, w2, b2, w3, decay):
    pos = np.arange(seq_len, dtype=np.float32)
    t = pos / np.float32(max(seq_len - 1, 1))
    ang = (np.float32(2.0 * math.pi) * pos / np.float32(seq_len)).astype(np.float32)
    bands = np.linspace(1e-4, HY_BANDS - 1, HY_BANDS, dtype=np.float32)
    fw = (ang[:, None] * bands[None, :]).astype(np.float32)
    z = jnp.concatenate([jnp.asarray(t)[:, None], jnp.cos(jnp.asarray(fw)), -jnp.sin(jnp.asarray(fw))], axis=-1)
    z = jnp.pad(z, ((0, 0), (0, LANES - HY_EMB)))
    padh = LANES - HY_FFN
    w1p = jnp.pad(w1, ((0, LANES - HY_EMB), (0, padh)))
    w2p = jnp.pad(w2, ((0, padh), (0, padh)))
    w3p = jnp.pad(w3, ((0, padh), (0, 0)))
    row = lambda v: jnp.pad(v, (0, padh)).reshape(1, LANES)
    n_out = w3.shape[1]
    tl = min(seq_len, 256)
    full = lambda shape: pl.BlockSpec(shape, lambda i: (0, 0))
    return pl.pallas_call(
        _hy_filter_kernel,
        grid=(seq_len // tl,),
        in_specs=[pl.BlockSpec((tl, LANES), lambda i: (i, 0)), full((LANES, LANES)), full((1, LANES)),
                  full((1, LANES)), full((LANES, LANES)), full((1, LANES)), full((LANES, n_out)), full((1, n_out))],
        out_specs=pl.BlockSpec((tl, n_out), lambda i: (i, 0)),
        out_shape=jax.ShapeDtypeStruct((seq_len, n_out), F32),
        compiler_params=_cparams(("arbitrary",)),
        name="hy_filter",
    )(z, w1p, row(b1), row(freq), w2p, row(b2), w3p, decay.reshape(1, n_out))


@functools.lru_cache(maxsize=None)
def _hy_constants(seq_len):
    nfft = 2 * seq_len
    n2, jj, g = HY_N2, HY_J, HY_G
    n1 = nfft // n2
    h1 = n1 // 2
    k1 = np.arange(n1)

    def kron_mat(rows_n, cols_n, sign, both_parts_in):
        m = np.zeros((rows_n, 2, jj, cols_n, 2 if both_parts_in else 1, jj))
        for j in range(jj):
            r = np.arange(rows_n)[:, None]
            c = np.arange(cols_n)[None, :]
            kk = r if sign < 0 else c
            ph = sign * 2.0 * np.pi * (r * c / n1 + kk * j / nfft)
            cr, ci = np.cos(ph), np.sin(ph)
            m[:, 0, j, :, 0, j] = cr
            m[:, 1, j, :, 0, j] = ci
            if both_parts_in:
                m[:, 0, j, :, 1, j] = -ci
                m[:, 1, j, :, 1, j] = cr
        return m.reshape(rows_n * 2 * jj, -1)

    ma = kron_mat(n1, h1, -1.0, True)
    mak = kron_mat(n1, h1, -1.0, False)
    mai = kron_mat(h1, n1, +1.0, True)

    nn = (np.arange(g)[:, None] * jj + np.arange(jj)[None, :])
    kk2 = np.arange(n2)
    ph = -2.0 * np.pi * kk2[:, None, None] * nn[None] / n2
    mc = np.zeros((2, n2, g, 2, jj))
    mc[0, :, :, 0, :] = np.cos(ph)
    mc[0, :, :, 1, :] = -np.sin(ph)
    mc[1, :, :, 0, :] = np.sin(ph)
    mc[1, :, :, 1, :] = np.cos(ph)
    mc = mc.reshape(2 * n2, 2 * n2)
    mci = np.zeros((g, 2, jj, 2, n2))
    phi = np.transpose(-ph, (1, 2, 0))
    mci[:, 0, :, 0, :] = np.cos(phi)
    mci[:, 0, :, 1, :] = -np.sin(phi)
    mci[:, 1, :, 0, :] = np.sin(phi)
    mci[:, 1, :, 1, :] = np.cos(phi)
    mci = mci.reshape(2 * n2, 2 * n2)

    th = 2.0 * np.pi * (np.arange(g)[:, None] * jj) * k1[None, :] / nfft
    twc = np.cos(th).astype(np.float32).reshape(-1)
    tws = (-np.sin(th)).astype(np.float32).reshape(-1)
    to_bf = lambda m: m.astype(np.float32).astype(BF16)
    return dict(ma=to_bf(ma), mak=to_bf(mak), mai=to_bf(mai), mc=to_bf(mc), mci=to_bf(mci), twc=twc, tws=tws,
                n1=n1, h1=h1)


def _hy_twiddle_tiles(y, g, twc_ref, tws_ref, n1, conj):
    tiles = []
    for k1 in range(n1):
        c = twc_ref[g * n1 + k1]
        s = tws_ref[g * n1 + k1]
        if conj:
            s = -s
        yr = y[HY_TILE * k1:HY_TILE * k1 + HY_J]
        yi = y[HY_TILE * k1 + HY_J:HY_TILE * (k1 + 1)]
        tiles.append(jnp.concatenate([c * yr - s * yi, s * yr + c * yi], axis=0).astype(BF16))
    return tiles


def _hy_tile_rows(k1, g):
    return pl.ds(pl.multiple_of(k1 * 2 * HY_N2 + g * HY_TILE, HY_TILE), HY_TILE)


def _hy_khat_kernel(twc_ref, tws_ref, hf_ref, hb_ref, mak_ref, mc_ref, kh_ref, a_ref, *, n1):
    nfft = n1 * HY_N2
    h1 = n1 // 2
    cb = hf_ref.shape[1]

    def stage_a(g, carry):
        def slab(ref, r):
            return ref[pl.ds(pl.multiple_of(r * HY_N2 + g * HY_J, HY_J), HY_J), :]

        fs = [slab(hf_ref, r) for r in range(h1)]
        bs = [slab(hb_ref, r) for r in range(h1)]
        first = (lax.broadcasted_iota(jnp.int32, (HY_J, cb), 0) == 0) & (g == 0)
        bs[0] = jnp.where(first, 0.0, bs[0])
        rhs = jnp.concatenate([jnp.concatenate(fs, axis=0), jnp.concatenate(bs, axis=0)], axis=1).astype(BF16)
        y = jnp.dot(mak_ref[...], rhs, preferred_element_type=F32)
        for k1, t in enumerate(_hy_twiddle_tiles(y, g, twc_ref, tws_ref, n1, False)):
            a_ref[_hy_tile_rows(k1, g), :] = t
        return carry

    lax.fori_loop(0, HY_G, stage_a, 0)

    kc = min(4, n1)

    def stage_c(kq, carry):
        blks = [pl.ds(pl.multiple_of((kq * kc + u) * 2 * HY_N2, 2 * HY_N2), 2 * HY_N2) for u in range(kc)]
        x = jnp.dot(mc_ref[...], jnp.concatenate([a_ref[b, :] for b in blks], axis=1),
                    preferred_element_type=F32) * (1.0 / nfft)
        for u, b in enumerate(blks):
            xf, xb = x[:, 2 * u * cb:(2 * u + 1) * cb], x[:, (2 * u + 1) * cb:(2 * u + 2) * cb]
            kh_ref[b, :] = jnp.concatenate([xf[:HY_N2] + xb[:HY_N2], xf[HY_N2:] - xb[HY_N2:]], axis=0).astype(BF16)
        return carry

    lax.fori_loop(0, n1 // kc, stage_c, 0)


def _hy_khat_call(h, consts):
    seq_len, c2 = h.shape
    c = c2 // 2
    n1 = consts["n1"]
    rows = n1 * 2 * HY_N2
    ncb = c // HY_CB
    return pl.pallas_call(
        functools.partial(_hy_khat_kernel, n1=n1),
        grid=(ncb,),
        in_specs=[pl.BlockSpec(memory_space=pltpu.SMEM), pl.BlockSpec(memory_space=pltpu.SMEM),
                  pl.BlockSpec((seq_len, HY_CB), lambda i: (0, i)),
                  pl.BlockSpec((seq_len, HY_CB), lambda i: (0, ncb + i)),
                  pl.BlockSpec(consts["mak"].shape, lambda i: (0, 0)),
                  pl.BlockSpec(consts["mc"].shape, lambda i: (0, 0))],
        out_specs=pl.BlockSpec((rows, HY_CB), lambda i: (0, i)),
        out_shape=jax.ShapeDtypeStruct((rows, c), BF16),
        scratch_shapes=[pltpu.VMEM((rows, 2 * HY_CB), BF16)],
        compiler_params=_cparams(("arbitrary",)),
        name="hy_khat",
    )(jnp.asarray(consts["twc"]), jnp.asarray(consts["tws"]), h, h, jnp.asarray(consts["mak"]),
      jnp.asarray(consts["mc"]))


def _hy_conv3_chunk(ref, base, w, b, r, n_chunks, seq_len):
    r0 = pl.multiple_of(r * HY_N2, HY_N2)
    cur = ref[pl.ds(base + r0, HY_N2), :].astype(F32)
    pr0 = pl.multiple_of(jnp.maximum(r0 - HY_TILE, 0), HY_TILE)
    nx0 = pl.multiple_of(jnp.minimum(r0 + HY_N2, seq_len - HY_TILE), HY_TILE)
    prev_row = ref[pl.ds(base + pr0, HY_TILE), :].astype(F32)[HY_TILE - 1:HY_TILE]
    next_row = ref[pl.ds(base + nx0, HY_TILE), :].astype(F32)[0:1]
    prev_row = jnp.where(r > 0, prev_row, 0.0)
    next_row = jnp.where(r < n_chunks - 1, next_row, 0.0)
    row = lax.broadcasted_iota(jnp.int32, cur.shape, 0)
    up = jnp.where(row == 0, prev_row, pltpu.roll(cur, 1, 0))
    dn = jnp.where(row == HY_N2 - 1, next_row, pltpu.roll(cur, HY_N2 - 1, 0))
    return up * w[0:1] + cur * w[1:2] + dn * w[2:3] + b


def _hy_interleave(a, b):
    parts = []
    for g in range(HY_G):
        parts.append(a[g * HY_J:(g + 1) * HY_J])
        parts.append(b[g * HY_J:(g + 1) * HY_J])
    return jnp.concatenate(parts, axis=0)


def _hy_conv_kernel(twc_ref, tws_ref, x0_ref, x1_ref, v_ref,
                    cw0_ref, cw1_ref, cw2_ref, cb0_ref, cb1_ref, cb2_ref, bias_ref, kh_ref,
                    ma_ref, mc_ref, mci_ref, mai_ref, o_ref, zc_ref, x0s_ref, a_ref, *, n1, seq_len):
    h1 = n1 // 2
    tile_rows = 2 * HY_N2

    def prologue(r, carry):
        cw0, cw1, cw2 = cw0_ref[...], cw1_ref[...], cw2_ref[...]
        cb0, cb1, cb2 = cb0_ref[...], cb1_ref[...], cb2_ref[...]
        zs, x0s = [], []
        for base in (0, seq_len):
            x0s.append(_hy_conv3_chunk(x0_ref, base, cw0, cb0, r, h1, seq_len))
            x1c = _hy_conv3_chunk(x1_ref, base, cw1, cb1, r, h1, seq_len)
            vc = _hy_conv3_chunk(v_ref, base, cw2, cb2, r, h1, seq_len)
            zs.append(x1c * vc)
        dst = pl.ds(pl.multiple_of(r * tile_rows, tile_rows), tile_rows)
        zc_ref[dst, :] = _hy_interleave(zs[0], zs[1]).astype(BF16)
        x0s_ref[dst, :] = _hy_interleave(x0s[0], x0s[1]).astype(BF16)
        return carry

    lax.fori_loop(0, h1, prologue, 0)

    cb = o_ref.shape[1]
    ga = 2
    kc = min(4, n1)

    def stage_a(gp, carry):
        groups = [gp * ga + u for u in range(ga)]
        rhs = jnp.concatenate(
            [jnp.concatenate([zc_ref[_hy_tile_rows(r, g), :] for r in range(h1)], axis=0) for g in groups], axis=1)
        y = jnp.dot(ma_ref[...], rhs, preferred_element_type=F32)
        for u, g in enumerate(groups):
            for k1, t in enumerate(_hy_twiddle_tiles(y[:, u * cb:(u + 1) * cb], g, twc_ref, tws_ref, n1, False)):
                a_ref[_hy_tile_rows(k1, g), :] = t
        return carry

    lax.fori_loop(0, HY_G // ga, stage_a, 0)

    def stage_c(kq, carry):
        blks = [pl.ds(pl.multiple_of((kq * kc + u) * tile_rows, tile_rows), tile_rows) for u in range(kc)]
        x = jnp.dot(mc_ref[...], jnp.concatenate([a_ref[b, :] for b in blks], axis=1), preferred_element_type=F32)
        pms = []
        for u, b in enumerate(blks):
            kh = kh_ref[b, :].astype(F32)
            xr, xi = x[:HY_N2, u * cb:(u + 1) * cb], x[HY_N2:, u * cb:(u + 1) * cb]
            kr, ki = kh[:HY_N2], kh[HY_N2:]
            pms.append(jnp.concatenate([xr * kr - xi * ki, xr * ki + xi * kr], axis=0).astype(BF16))
        z = jnp.dot(mci_ref[...], jnp.concatenate(pms, axis=1), preferred_element_type=F32).astype(BF16)
        for u, b in enumerate(blks):
            a_ref[b, :] = z[:, u * cb:(u + 1) * cb]
        return carry

    lax.fori_loop(0, n1 // kc, stage_c, 0)

    bias = bias_ref[...]

    def stage_ai(gp, carry):
        groups = [gp * ga + u for u in range(ga)]
        cols = []
        for g in groups:
            tiles = []
            for k1 in range(n1):
                t = a_ref[_hy_tile_rows(k1, g), :].astype(F32)
                c = twc_ref[g * n1 + k1]
                s = -tws_ref[g * n1 + k1]
                tr, ti = t[:HY_J], t[HY_J:]
                tiles.append(jnp.concatenate([c * tr - s * ti, s * tr + c * ti], axis=0).astype(BF16))
            cols.append(jnp.concatenate(tiles, axis=0))
        y = jnp.dot(mai_ref[...], jnp.concatenate(cols, axis=1), preferred_element_type=F32)
        for u, g in enumerate(groups):
            for r in range(h1):
                src = _hy_tile_rows(r, g)
                z = zc_ref[src, :].astype(F32)
                x0 = x0s_ref[src, :].astype(F32)
                o = x0 * (y[HY_TILE * r:HY_TILE * (r + 1), u * cb:(u + 1) * cb] + bias * z)
                dst = pl.ds(pl.multiple_of(r * HY_N2 + g * HY_J, HY_J), HY_J)
                o_ref[dst, :] = o[:HY_J]
                dst2 = pl.ds(pl.multiple_of(seq_len + r * HY_N2 + g * HY_J, HY_J), HY_J)
                o_ref[dst2, :] = o[HY_J:]
        return carry

    lax.fori_loop(0, HY_G // ga, stage_ai, 0)


def _hy_conv_call(p, khat, conv_w, conv_b, bias, consts, *, n_batch, seq_len, row0):
    c = HY_WIDTH
    n1 = consts["n1"]
    h1 = consts["h1"]
    rb0 = row0 // (2 * seq_len)
    ncb = c // HY_CB
    tile_rows = 2 * HY_N2

    def pin(col):
        return pl.BlockSpec((2 * seq_len, HY_CB), lambda cb, pi: (rb0 + pi, col // HY_CB + cb))

    def cws(stream):
        return pl.BlockSpec((3, HY_CB), lambda cb, pi: (0, stream * ncb + cb))

    def cbs(stream):
        return pl.BlockSpec((1, HY_CB), lambda cb, pi: (0, stream * ncb + cb))

    const = lambda a: pl.BlockSpec(a.shape, lambda cb, pi: (0, 0))
    mats = [jnp.asarray(consts[k]) for k in ("ma", "mc", "mci", "mai")]
    in_specs = [pl.BlockSpec(memory_space=pltpu.SMEM), pl.BlockSpec(memory_space=pltpu.SMEM),
                pin(COL_HY_X0), pin(COL_HY_X1), pin(COL_HY_V), cws(0), cws(1), cws(2), cbs(0), cbs(1), cbs(2),
                pl.BlockSpec((1, HY_CB), lambda cb, pi: (0, cb)),
                pl.BlockSpec((n1 * tile_rows, HY_CB), lambda cb, pi: (0, cb))] + [const(m) for m in mats]
    return pl.pallas_call(
        functools.partial(_hy_conv_kernel, n1=n1, seq_len=seq_len),
        grid=(ncb, n_batch // 2),
        in_specs=in_specs,
        out_specs=pl.BlockSpec((2 * seq_len, HY_CB), lambda cb, pi: (pi, cb)),
        out_shape=jax.ShapeDtypeStruct((n_batch * seq_len, c), F32),
        scratch_shapes=[pltpu.VMEM((h1 * tile_rows, HY_CB), BF16), pltpu.VMEM((h1 * tile_rows, HY_CB), BF16),
                        pltpu.VMEM((n1 * tile_rows, HY_CB), BF16)],
        compiler_params=_cparams(("arbitrary", "arbitrary"), 56),
        name="hy_conv",
    )(jnp.asarray(consts["twc"]), jnp.asarray(consts["tws"]), p, p, p,
      conv_w, conv_w, conv_w, conv_b.reshape(1, -1), conv_b.reshape(1, -1), conv_b.reshape(1, -1),
      bias.reshape(1, -1), khat, *mats)


def _hyena_layer(p, params, *, n_batch, seq_len, row0):
    conv_w, conv_b, w1, b1, freq, w2, b2, w3, decay, bias = params
    consts = _hy_constants(seq_len)
    h = _hy_filter_call(seq_len, w1, b1, freq, w2, b2, w3, decay)
    khat = _hy_khat_call(h, consts)
    return _hy_conv_call(p, khat, conv_w, conv_b, bias, consts, n_batch=n_batch, seq_len=seq_len, row0=row0)


ROW_WORDS = D_MODEL // 2
ROW_SUB = ROW_WORDS // LANES


def _pack_rows(val, ref2d, n_rows):
    lo = lax.bitcast_convert_type(val[:, :ROW_WORDS].astype(BF16).astype(F32), jnp.uint32)
    hi = lax.bitcast_convert_type(val[:, ROW_WORDS:].astype(BF16).astype(F32), jnp.uint32)
    word = (hi & jnp.uint32(0xFFFF0000)) | (lo >> 16)
    for s in range(ROW_SUB):
        ref2d[pl.ds(s, n_rows, stride=ROW_SUB), :] = word[:, s * LANES:(s + 1) * LANES]


def _unpack_words(w):
    lo = lax.bitcast_convert_type(w << 16, F32)
    hi = lax.bitcast_convert_type(w & jnp.uint32(0xFFFF0000), F32)
    return lo, hi


def _merge_kernel(*refs, has_ctx, n_lat_tiles, tm):
    if has_ctx:
        (x_ref, g1_ref, sh2_ref, sc2_ref, nw2_ref, gl0_ref, gl1_ref, gl2_ref, ya_l, yb_l, yc_l, ya_c, yb_c, yc_c,
         wb_ref, wo_ref, rw_ref, rb_ref, xo_ref, h2p_ref, lg_ref) = refs
    else:
        (x_ref, g1_ref, sh2_ref, sc2_ref, nw2_ref, gl0_ref, gl1_ref, gl2_ref, ya_l, yb_l, yc_l,
         wb_ref, wo_ref, rw_ref, rb_ref, xo_ref, h2p_ref, lg_ref) = refs
    ys = [ya_l[...].astype(BF16), yb_l[...], yc_l[...]]
    if has_ctx:
        is_ctx = pl.program_id(0) >= n_lat_tiles
        ys = [jnp.where(is_ctx, c, l) for c, l in zip((ya_c[...].astype(BF16), yb_c[...], yc_c[...]), ys)]
    m = None
    for g, gl_ref in enumerate((gl0_ref, gl1_ref, gl2_ref)):
        proj = jnp.dot(ys[g], wb_ref[g], preferred_element_type=F32)
        term = jax.nn.sigmoid(gl_ref[...].astype(F32)) * proj
        m = term if m is None else m + term
    out = jnp.dot(m.astype(BF16), wo_ref[...], preferred_element_type=F32)
    xn = x_ref[...] + g1_ref[...] * out
    xo_ref[...] = xn
    h2 = _rms_modulate(xn, nw2_ref[...], sh2_ref[...], 1.0 + sc2_ref[...])
    lg_ref[...] = _dot3(h2, rw_ref[...]) + rb_ref[...]
    _pack_rows(h2, h2p_ref, tm)


def _merge_call(xf, p, mods_l, norm2_w, ys_lat, ys_ctx, wb_bf, wo_bf, rw_pad, rb_pad, *, n_batch, n_seq, n_rows):
    d = D_MODEL
    has_ctx = ys_ctx is not None
    tm = 256
    tpb = n_seq // tm
    nlt = n_batch * tpb
    nt = n_rows // tm

    def mod_spec(comp):
        return pl.BlockSpec((None, None, 1, d), lambda i: (_mod_row(i, tpb, nlt, n_batch), comp, 0, 0))

    def gate_spec(g):
        return pl.BlockSpec((tm, d), lambda i: (i, COL_GATES // d + g))

    lat_spec = pl.BlockSpec((tm, MIX_WIDTH), lambda i: (jnp.minimum(i, nlt - 1), 0))
    ctx_spec = pl.BlockSpec((tm, MIX_WIDTH), lambda i: (jnp.maximum(i - nlt, 0), 0))
    once = pl.Buffered(1)
    in_specs = [pl.BlockSpec((tm, d), lambda i: (i, 0)), mod_spec(2), mod_spec(3), mod_spec(4),
                pl.BlockSpec((1, d), lambda i: (0, 0)), gate_spec(0), gate_spec(1), gate_spec(2),
                lat_spec, lat_spec, lat_spec]
    args = [xf, mods_l, mods_l, mods_l, norm2_w.reshape(1, d), p, p, p, *ys_lat]
    if has_ctx:
        in_specs += [ctx_spec, ctx_spec, ctx_spec]
        args += list(ys_ctx)
    in_specs += [pl.BlockSpec((N_BRANCH, MIX_WIDTH, d), lambda i: (0, 0, 0), pipeline_mode=once),
                 pl.BlockSpec((d, d), lambda i: (0, 0), pipeline_mode=once),
                 pl.BlockSpec((d, LANES), lambda i: (0, 0)), pl.BlockSpec((1, LANES), lambda i: (0, 0))]
    args += [wb_bf, wo_bf, rw_pad, rb_pad]
    return pl.pallas_call(
        functools.partial(_merge_kernel, has_ctx=has_ctx, n_lat_tiles=nlt, tm=tm),
        grid=(nt,),
        in_specs=in_specs,
        out_specs=[pl.BlockSpec((tm, d), lambda i: (i, 0)),
                   pl.BlockSpec((tm * ROW_SUB, LANES), lambda i: (i, 0)),
                   pl.BlockSpec((tm, LANES), lambda i: (i, 0))],
        out_shape=[jax.ShapeDtypeStruct((n_rows, d), F32),
                   jax.ShapeDtypeStruct((n_rows * ROW_SUB, LANES), jnp.uint32),
                   jax.ShapeDtypeStruct((n_rows, LANES), F32)],
        compiler_params=_cparams(("arbitrary",), 58),
        name="merge",
    )(*args)


def _route_kernel(lg_ref, idx_ref, rank_ref, gate_ref, cnt_ref, carry_ref, *, tr):
    i = pl.program_id(0)

    @pl.when(i == 0)
    def _():
        carry_ref[...] = jnp.zeros_like(carry_ref)

    lane = lax.broadcasted_iota(jnp.int32, (tr, LANES), 1)
    lg = jnp.where(lane < N_EXPERTS, lg_ref[...], -jnp.inf)
    tops, hots = [], []
    for _ in range(TOP_K):
        mx = jnp.max(lg, axis=-1, keepdims=True)
        sel = jnp.min(jnp.where(lg == mx, lane, LANES), axis=-1, keepdims=True)
        hot = lane == sel
        tops.append((mx, sel))
        hots.append(hot)
        lg = jnp.where(hot, -jnp.inf, lg)
    es = [jnp.exp(mx - tops[0][0]) for mx, _ in tops]
    den = es[0] + es[1] + es[2] + es[3]
    hot_f = [jnp.where(h, 1.0, 0.0) for h in hots]
    chosen = hot_f[0] + hot_f[1] + hot_f[2] + hot_f[3]
    r_i = lax.broadcasted_iota(jnp.int32, (tr, tr), 0)
    c_i = lax.broadcasted_iota(jnp.int32, (tr, tr), 1)
    below = jnp.where(c_i < r_i, 1.0, 0.0).astype(BF16)
    before = jnp.dot(below, chosen.astype(BF16), preferred_element_type=F32) + carry_ref[...]
    idx_o = jnp.zeros((tr, LANES), jnp.int32)
    rank_o = jnp.zeros((tr, LANES), jnp.int32)
    gate_o = jnp.zeros((tr, LANES), F32)
    for j in range(TOP_K):
        rank_j = jnp.sum(hot_f[j] * before, axis=-1, keepdims=True).astype(jnp.int32)
        idx_o = jnp.where(lane == j, tops[j][1], idx_o)
        rank_o = jnp.where(lane == j, rank_j, rank_o)
        gate_o = jnp.where(lane == j, es[j] / den, gate_o)
    idx_ref[...] = idx_o
    rank_ref[...] = rank_o
    gate_ref[...] = gate_o
    carry_ref[...] += jnp.sum(chosen, axis=0, keepdims=True)
    cnt_ref[...] = carry_ref[...]


def _route_call(logits):
    t_rows = logits.shape[0]
    tr = 512
    tile = pl.BlockSpec((tr, LANES), lambda i: (i, 0))
    return pl.pallas_call(
        functools.partial(_route_kernel, tr=tr),
        grid=(t_rows // tr,),
        in_specs=[tile],
        out_specs=[tile, tile, tile, pl.BlockSpec((1, LANES), lambda i: (0, 0))],
        out_shape=[jax.ShapeDtypeStruct((t_rows, LANES), jnp.int32), jax.ShapeDtypeStruct((t_rows, LANES), jnp.int32),
                   jax.ShapeDtypeStruct((t_rows, LANES), F32), jax.ShapeDtypeStruct((1, LANES), F32)],
        scratch_shapes=[pltpu.VMEM((1, LANES), F32)],
        compiler_params=_cparams(("arbitrary",)),
        name="route",
    )(logits)


MOE_BLK = 256


def _row_tile(ref2d, row):
    return ref2d.at[pl.ds(pl.multiple_of(row * ROW_SUB, ROW_SUB), ROW_SUB), :]


def _dispatch_kernel(dest_ref, cnt_ref, pad_ref, off_ref, h_ref, xs_ref, zero_ref, sem, zsem, *, td):
    i = pl.program_id(0)

    @pl.when(i == 0)
    def _():
        zero_ref[...] = jnp.zeros_like(zero_ref)
        for e in range(N_EXPERTS):
            base = off_ref[e]

            def fill(r, carry):
                pltpu.make_async_copy(zero_ref, _row_tile(xs_ref, base + r), zsem).start()
                return carry

            def drain(r, carry):
                pltpu.make_async_copy(zero_ref, _row_tile(xs_ref, base + r), zsem).wait()
                return carry

            lax.fori_loop(cnt_ref[e], pad_ref[e], fill, 0)
            lax.fori_loop(cnt_ref[e], pad_ref[e], drain, 0)

    def issue(r, carry):
        src = _row_tile(h_ref, r)
        for j in range(TOP_K):
            pltpu.make_async_copy(src, _row_tile(xs_ref, dest_ref[r * TOP_K + j]), sem).start(priority=j % 2)
        return carry

    lax.fori_loop(0, td, issue, 0)

    def drain_all(r, carry):
        src = _row_tile(h_ref, r)
        for j in range(TOP_K):
            pltpu.make_async_copy(src, _row_tile(xs_ref, dest_ref[r * TOP_K + j]), sem).wait()
        return carry

    lax.fori_loop(0, td, drain_all, 0)


def _dispatch_call(h2p, dest_flat, counts, padded, offsets, n_rows_sorted):
    t_rows = h2p.shape[0] // ROW_SUB
    td = 256
    smem = pl.BlockSpec(memory_space=pltpu.SMEM)
    return pl.pallas_call(
        functools.partial(_dispatch_kernel, td=td),
        grid=(t_rows // td,),
        in_specs=[pl.BlockSpec((td * TOP_K,), lambda i: (i,), memory_space=pltpu.SMEM), smem, smem, smem,
                  pl.BlockSpec((td * ROW_SUB, LANES), lambda i: (i, 0))],
        out_specs=pl.BlockSpec(memory_space=pl.ANY),
        out_shape=jax.ShapeDtypeStruct((n_rows_sorted * ROW_SUB, LANES), jnp.uint32),
        scratch_shapes=[pltpu.VMEM((ROW_SUB, LANES), jnp.uint32), pltpu.SemaphoreType.DMA(()),
                        pltpu.SemaphoreType.DMA(())],
        compiler_params=_cparams(("arbitrary",)),
        name="moe_dispatch",
    )(dest_flat, counts, padded, offsets, h2p)


def _expert_kernel(row0_ref, nblk_ref, x_hbm, w1_ref, b1_ref, w2_ref, b2_ref, y_hbm, xbuf_ref, ybuf_ref,
                   w1s_ref, w2s_ref, xsem, ysem):
    e = pl.program_id(0)
    nb = nblk_ref[e]
    row0 = row0_ref[e]
    blk_rows = MOE_BLK * ROW_SUB

    def hbm_rows(ref, k):
        return ref.at[pl.ds(pl.multiple_of((row0 + k * MOE_BLK) * ROW_SUB, blk_rows), blk_rows), :]

    def x_copy(k, slot):
        return pltpu.make_async_copy(hbm_rows(x_hbm, k), xbuf_ref.at[slot], xsem.at[slot])

    def y_copy(k, slot):
        return pltpu.make_async_copy(ybuf_ref.at[slot], hbm_rows(y_hbm, k), ysem.at[slot])

    def compute(slot):
        pieces = []
        for s in range(ROW_SUB):
            lo, hi = _unpack_words(xbuf_ref[slot, pl.ds(s, MOE_BLK, stride=ROW_SUB), :])
            pieces += [lo.astype(BF16), hi.astype(BF16)]
        xc = jnp.concatenate(pieces, axis=1)
        a = jnp.dot(xc, w1s_ref[...], preferred_element_type=F32) + b1_ref[...]
        a_glu = jnp.minimum(a[:, :EXPERT_FF], SWIGLU_LIMIT)
        a_lin = jnp.clip(a[:, EXPERT_FF:], -SWIGLU_LIMIT, SWIGLU_LIMIT)
        act = a_glu * jax.nn.sigmoid(SWIGLU_ALPHA * a_glu) * (a_lin + 1.0)
        y = jnp.dot(act.astype(BF16), w2s_ref[...], preferred_element_type=F32) + b2_ref[...]
        _pack_rows(y, ybuf_ref.at[slot], MOE_BLK)

    @pl.when(nb > 0)
    def _():
        x_copy(0, 0).start()
        for s in range(ROW_SUB):
            r = 2 * s * LANES
            w1s_ref[r:r + LANES, :] = w1_ref[s * LANES:(s + 1) * LANES, :].astype(BF16)
            w1s_ref[r + LANES:r + 2 * LANES, :] = w1_ref[ROW_WORDS + s * LANES:ROW_WORDS + (s + 1) * LANES, :].astype(BF16)
        w2s_ref[...] = w2_ref[...].astype(BF16)

        def pair(m, carry):
            for slot in range(2):
                k = 2 * m + slot

                @pl.when(k < nb)
                def _():
                    @pl.when(k + 1 < nb)
                    def _():
                        x_copy(k + 1, 1 - slot).start()

                    x_copy(k, slot).wait()

                    @pl.when(k >= 2)
                    def _():
                        y_copy(k - 2, slot).wait()

                    compute(slot)
                    y_copy(k, slot).start()
            return carry

        lax.fori_loop(0, (nb + 1) // 2, pair, 0)
        for slot in range(2):
            last = jnp.where((nb - 1) % 2 == slot, nb - 1, nb - 2)

            @pl.when(last >= 0)
            def _():
                y_copy(last, slot).wait()


def _expert_call(xs, row0, nblk, w1, b1, w2, b2, layer):
    depth, n_exp, d, ff2 = w1.shape
    ff = w2.shape[2]
    blk_rows = MOE_BLK * ROW_SUB

    def wspec(r, c):
        return pl.BlockSpec((None, None, r, c), lambda e, r0, nb: (layer, e, 0, 0))

    buf = pltpu.VMEM((2, blk_rows, LANES), jnp.uint32)
    return pl.pallas_call(
        _expert_kernel,
        grid_spec=pltpu.PrefetchScalarGridSpec(
            num_scalar_prefetch=2,
            grid=(n_exp,),
            in_specs=[pl.BlockSpec(memory_space=pl.ANY),
                      wspec(d, ff2), wspec(1, ff2), wspec(ff, d), wspec(1, d)],
            out_specs=pl.BlockSpec(memory_space=pl.ANY),
            scratch_shapes=[buf, buf, pltpu.VMEM((d, ff2), BF16), pltpu.VMEM((ff, d), BF16),
                            pltpu.SemaphoreType.DMA((2,)), pltpu.SemaphoreType.DMA((2,))],
        ),
        out_shape=jax.ShapeDtypeStruct(xs.shape, jnp.uint32),
        compiler_params=_cparams(("arbitrary",), 56),
        name="moe_experts",
    )(row0, nblk, xs, w1, b1.reshape(depth, n_exp, 1, ff2), w2, b2.reshape(depth, n_exp, 1, d))


def _combine_kernel(dest_ref, dest_next_ref, gate_ref, x_ref, g2_ref, fw_ref, ys_ref, o_ref, buf0_ref, buf1_ref,
                    sems, *, tc, final_norm):
    i = pl.program_id(0)
    n = pl.num_programs(0)
    bufs = (buf0_ref, buf1_ref)

    def row_copies(d_ref, b, r):
        return [pltpu.make_async_copy(_row_tile(ys_ref, d_ref[r * TOP_K + j]), _row_tile(bufs[b], j * tc + r),
                                      sems.at[b]) for j in range(TOP_K)]

    def start_tile(d_ref, b):
        def body(r, carry):
            for j, cp in enumerate(row_copies(d_ref, b, r)):
                cp.start(priority=j % 2)
            return carry

        lax.fori_loop(0, tc, body, 0)

    def wait_tile(b):
        def body(r, carry):
            for cp in row_copies(dest_ref, b, r):
                cp.wait()
            return carry

        lax.fori_loop(0, tc, body, 0)

    def reduce_tile(buf_ref):
        gate = gate_ref[...]
        gs = [gate[:, j:j + 1] for j in range(TOP_K)]
        lo_parts, hi_parts = [], []
        for s in range(ROW_SUB):
            f_lo = None
            f_hi = None
            for j in range(TOP_K):
                lo, hi = _unpack_words(buf_ref[pl.ds(j * tc * ROW_SUB + s, tc, stride=ROW_SUB), :])
                f_lo = gs[j] * lo if f_lo is None else f_lo + gs[j] * lo
                f_hi = gs[j] * hi if f_hi is None else f_hi + gs[j] * hi
            lo_parts.append(f_lo)
            hi_parts.append(f_hi)
        f = jnp.concatenate(lo_parts + hi_parts, axis=1)
        xn = x_ref[...] + g2_ref[...] * f
        if final_norm:
            ms = jnp.mean(xn * xn, axis=-1, keepdims=True)
            xn = xn * lax.rsqrt(ms + NORM_EPS) * fw_ref[...]
        o_ref[...] = xn

    @pl.when(i == 0)
    def _():
        start_tile(dest_ref, 0)

    for b in range(2):
        @pl.when((i % 2 == b) & (i + 1 < n))
        def _():
            start_tile(dest_next_ref, 1 - b)

    for b in range(2):
        @pl.when(i % 2 == b)
        def _():
            wait_tile(b)
            reduce_tile(bufs[b])


def _combine_call(xf, ys, dest_flat, gates, mods_l, final_w, *, n_batch, n_seq, final_norm):
    t_rows, d = xf.shape
    tc = 256
    tpb = n_seq // tc
    nlt = n_batch * tpb
    nt = t_rows // tc
    buf = pltpu.VMEM((TOP_K * tc * ROW_SUB, LANES), jnp.uint32)
    return pl.pallas_call(
        functools.partial(_combine_kernel, tc=tc, final_norm=final_norm),
        grid=(nt,),
        in_specs=[pl.BlockSpec((tc * TOP_K,), lambda i: (i,), memory_space=pltpu.SMEM),
                  pl.BlockSpec((tc * TOP_K,), lambda i: (jnp.minimum(i + 1, nt - 1),), memory_space=pltpu.SMEM),
                  pl.BlockSpec((tc, LANES), lambda i: (i, 0)),
                  pl.BlockSpec((tc, d), lambda i: (i, 0)),
                  pl.BlockSpec((None, None, 1, d), lambda i: (_mod_row(i, tpb, nlt, n_batch), 5, 0, 0)),
                  pl.BlockSpec((1, d), lambda i: (0, 0)),
                  pl.BlockSpec(memory_space=pl.ANY)],
        out_specs=pl.BlockSpec((tc, d), lambda i: (i, 0)),
        out_shape=jax.ShapeDtypeStruct((t_rows, d), F32),
        scratch_shapes=[buf, buf, pltpu.SemaphoreType.DMA((2,))],
        compiler_params=_cparams(("arbitrary",)),
        name="moe_combine",
    )(dest_flat, dest_flat, gates, xf, mods_l, final_w.reshape(1, d), ys)


def _moe_layer(xf, h2p, logits, mods_l, w1, b1, w2, b2, layer, final_w, *, n_batch, n_seq, final_norm):
    t_rows = xf.shape[0]
    idx, rank, gates, cnt = _route_call(logits)
    counts = cnt[0, :N_EXPERTS].astype(jnp.int32)
    padded = (counts + MOE_BLK - 1) // MOE_BLK * MOE_BLK
    ends = jnp.cumsum(padded)
    offsets = ends - padded
    dest = (offsets[idx[:, :TOP_K]] + rank[:, :TOP_K]).reshape(-1)
    n_blocks = (t_rows * TOP_K + N_EXPERTS * (MOE_BLK - 1)) // MOE_BLK
    xs = _dispatch_call(h2p, dest, counts, padded, offsets, n_blocks * MOE_BLK)
    ys = _expert_call(xs, offsets, padded // MOE_BLK, w1, b1, w2, b2, layer)
    return _combine_call(xf, ys, dest, gates, mods_l, final_w, n_batch=n_batch, n_seq=n_seq, final_norm=final_norm)


W_IN_PIECES = ((0, 4096, COL_HY_X0), (5376, 7424, COL_GLA_V), (4352, 5376, COL_GLA_Q), (4096, 4352, COL_ATT_K),
               (7424, 7456, COL_DECAY))
W_IN_GATES_SRC = 7456
W_IN_COLS = 13600


def _w_prep_kernel(w_ref, o_ref):
    for lo, hi, dst in W_IN_PIECES:
        o_ref[:, dst:dst + hi - lo] = w_ref[:, lo:hi].astype(BF16)
    shift = W_IN_GATES_SRC % LANES
    base = W_IN_GATES_SRC - shift
    width = N_BRANCH * D_MODEL
    step = 512
    for c in range(0, width, step):
        end = min(base + c + step + LANES, W_IN_COLS)
        win = w_ref[:, base + c:end]
        o_ref[:, COL_GATES + c:COL_GATES + c + step] = win[:, shift:shift + step].astype(BF16)
    tail = COL_DECAY + GLA_RANK * 2
    o_ref[:, tail:] = jnp.zeros((o_ref.shape[0], P_PAD - tail), BF16)


def _permute_w_in(w_in, layer):
    d = w_in.shape[1]
    tr = 128
    return pl.pallas_call(
        _w_prep_kernel,
        grid=(d // tr,),
        in_specs=[pl.BlockSpec((None, tr, W_IN_COLS), lambda i: (layer, i, 0))],
        out_specs=pl.BlockSpec((tr, P_PAD), lambda i: (i, 0)),
        out_shape=jax.ShapeDtypeStruct((d, P_PAD), BF16),
        compiler_params=_cparams(("arbitrary",)),
        name="w_prep",
    )(w_in)


def kernel(x, c, ctx, c_ctx, ada_w, ada_b, norm1_w, norm2_w, w_in, hy_conv_w, hy_conv_b, hy_filt_w1, hy_filt_b1,
           hy_filt_freq, hy_filt_w2, hy_filt_b2, hy_filt_w3, hy_decay, hy_bias, attn_sink, gla_a_w, gla_a_b,
           gla_norm_w, w_branch, w_out, router_w, router_b, moe_w1, moe_b1, moe_w2, moe_b2, final_norm_w):
    n_batch, n_seq, d = x.shape
    n_ctx = ctx.shape[1]
    depth = ada_w.shape[0]
    t_lat = n_batch * n_seq
    xf = jnp.concatenate([x.reshape(t_lat, d), ctx.reshape(n_batch * n_ctx, d)], axis=0)
    cond = jnp.zeros((SUBLANES, d), F32).at[:n_batch].set(c).at[n_batch].set(c_ctx)
    mods = _ada_call(cond, ada_w, ada_b).reshape(depth, SUBLANES, N_ADA, 1, d)
    rw_pad = jnp.pad(router_w, ((0, 0), (0, 0), (0, LANES - N_EXPERTS)))
    rb_pad = jnp.pad(router_b, ((0, 0), (0, LANES - N_EXPERTS))).reshape(depth, 1, LANES)
    cos_t, sin_t = _rope_tables(n_seq)
    for l in range(depth):
        need_ctx = l < depth - 1
        last = l == depth - 1
        mods_l = mods[l]
        p = _in_call(xf, norm1_w[l], mods_l, _permute_w_in(w_in, l), n_batch=n_batch, n_seq=n_seq)
        hy_params = (hy_conv_w[l], hy_conv_b[l], hy_filt_w1[l], hy_filt_b1[l], hy_filt_freq[l], hy_filt_w2[l],
                     hy_filt_b2[l], hy_filt_w3[l], hy_decay[l], hy_bias[l])
        y_a = _hyena_layer(p, hy_params, n_batch=n_batch, seq_len=n_seq, row0=0)
        kvp = _kv_prep_call(p, cos_t, sin_t, n_batch=n_batch, n_seq=n_seq)
        y_b = _attn_call(attn_sink[l], p, kvp, cos_t, sin_t, n_batch=n_batch, n_seq=n_seq, n_ctx=n_ctx,
                         q_row0=0, ctx_row0=t_lat, has_local=True)
        y_c, y_cc = _gla_layer(p, gla_a_w[l], gla_a_b[l], gla_norm_w[l], n_batch=n_batch, n_seq=n_seq,
                               n_ctx=n_ctx, need_ctx=need_ctx)
        ys_ctx = None
        if need_ctx:
            y_ac = _hyena_layer(p, hy_params, n_batch=n_batch, seq_len=n_ctx, row0=t_lat)
            y_bc = _attn_call(attn_sink[l], p, kvp, None, None, n_batch=n_batch, n_seq=n_ctx, n_ctx=n_ctx,
                              q_row0=t_lat, ctx_row0=t_lat, has_local=False)
            ys_ctx = (y_ac, y_bc, y_cc)
        n_rows = xf.shape[0] if need_ctx else t_lat
        xf, h2p, logits = _merge_call(xf, p, mods_l, norm2_w[l], (y_a, y_b, y_c), ys_ctx,
                                      w_branch[l].astype(BF16), w_out[l].astype(BF16),
                                      rw_pad[l], rb_pad[l], n_batch=n_batch, n_seq=n_seq, n_rows=n_rows)
        xf = _moe_layer(xf, h2p, logits, mods_l, moe_w1, moe_b1, moe_w2, moe_b2, l, final_norm_w,
                        n_batch=n_batch, n_seq=n_seq, final_norm=last)
    return xf.reshape(n_batch, n_seq, d)
```
